```python
import functools
import jax, jax.numpy as jnp
from jax import lax
import numpy as np

D_MODEL = 2048
BATCH = 2
SEQ = 4096
DEPTH = 2
DEC_BATCH = 32
DEC_SEQ = 4
PAST_LEN = 8192
PAGE_SIZE = 128

D_MIX = D_MODEL
N_HEADS = 8
HEAD_DIM = 128
W_ATTN = N_HEADS * HEAD_DIM
W_CONV = D_MIX // 4
W_POOL = D_MIX - W_ATTN - W_CONV
CONV_WIDTH = 31
CONV_CTX = CONV_WIDTH - 1
POOL_WINDOWS = (2, 4, 8, 16)
POOL_GROUP = W_POOL // len(POOL_WINDOWS)
POOL_CTX = max(POOL_WINDOWS) - 1
MOBA_BLOCK = 256
MOBA_TOPK = 3
Q_CHUNK = 128
ROPE_THETA = 10000.0
EPS = 1e-6
D_IN = 3 * W_CONV + 2 * W_POOL + 4 * W_ATTN

kernel_name = 'hymba_conv_pool_moba_step'


def rms_norm(x, g):
    xf = x.astype(jnp.float32)
    y = xf * lax.rsqrt(jnp.mean(xf * xf, axis=-1, keepdims=True) + EPS)
    return (y * g.astype(jnp.float32)).astype(x.dtype)


def layer_norm(x, g, b):
    xf = x.astype(jnp.float32)
    mu = jnp.mean(xf, axis=-1, keepdims=True)
    var = jnp.mean(jnp.square(xf - mu), axis=-1, keepdims=True)
    y = (xf - mu) * lax.rsqrt(var + EPS) * g.astype(jnp.float32) + b.astype(jnp.float32)
    return y.astype(x.dtype)


def rope(x, pos):
    half = HEAD_DIM // 2
    inv = ROPE_THETA ** (-jnp.arange(half, dtype=jnp.float32) / half)
    ang = pos.astype(jnp.float32)[:, None] * inv[None, :]
    cos = jnp.cos(ang)[None, :, None, :]
    sin = jnp.sin(ang)[None, :, None, :]
    xf = x.astype(jnp.float32)
    x1, x2 = xf[..., :half], xf[..., half:]
    return jnp.concatenate([x1 * cos - x2 * sin, x2 * cos + x1 * sin], axis=-1).astype(x.dtype)


def conv_branch(a, b, ctx, conv_w, conv_b, ln_g, ln_b):
    glu = a * jax.nn.sigmoid(b)
    u = jnp.concatenate([ctx.astype(glu.dtype), glu], axis=1)
    y = lax.conv_general_dilated(u, conv_w[:, None, :], window_strides=(1,), padding='VALID',
                                 dimension_numbers=('NWC', 'WIO', 'NWC'),
                                 feature_group_count=W_CONV) + conv_b
    y = jax.nn.silu(layer_norm(y, ln_g, ln_b))
    return y, u[:, -CONV_CTX:]


def pool_branch(u, ctx, pos0, pool_w, pool_scale):
    N, T, _ = u.shape
    uu = jnp.concatenate([ctx.astype(u.dtype), u], axis=1)
    uf = uu.astype(jnp.float32)
    csum = jnp.pad(jnp.cumsum(uf, axis=1), ((0, 0), (1, 0), (0, 0)))
    t = jnp.arange(T)
    top = csum[:, POOL_CTX + 1:POOL_CTX + 1 + T]
    tok = uf[:, POOL_CTX:]
    parts = []
    for g, w in enumerate(POOL_WINDOWS):
        sl = slice(g * POOL_GROUP, (g + 1) * POOL_GROUP)
        low = csum[:, POOL_CTX + 1 - w:POOL_CTX + 1 - w + T, sl]
        cnt = jnp.minimum(w, pos0 + t + 1).astype(jnp.float32)[None, :, None]
        parts.append((top[..., sl] - low) / cnt - tok[..., sl])
    p = jnp.stack(parts, axis=2).astype(u.dtype)
    y = jnp.einsum('btgc,gcd->btgd', p, pool_w).reshape(N, T, W_POOL) * pool_scale
    return y, uu[:, -POOL_CTX:]


def moba_prompt(q, k, v):
    B, T = q.shape[0], q.shape[1]
    nb = -(-T // MOBA_BLOCK)
    pad = nb * MOBA_BLOCK - T
    k_eff = min(MOBA_TOPK, nb)
    scale = HEAD_DIM ** -0.5
    qh = q.transpose(0, 2, 1, 3)

    def blocks(a):
        a = jnp.pad(a, ((0, 0), (0, pad), (0, 0), (0, 0))).transpose(0, 2, 1, 3)
        return a.reshape(B, N_HEADS, nb, MOBA_BLOCK, HEAD_DIM)

    kb, vb = blocks(k), blocks(v)
    kmean = jnp.mean(kb.astype(jnp.float32), axis=3)
    bi = jnp.arange(B)[:, None, None]
    hi = jnp.arange(N_HEADS)[None, :, None]

    def chunk(c):
        q0 = c * Q_CHUNK
        qc = lax.dynamic_slice_in_dim(qh, q0, Q_CHUNK, axis=2)
        qpos = q0 + jnp.arange(Q_CHUNK)
        own = q0 // MOBA_BLOCK
        gate = jnp.einsum('bhqd,bhnd->bhqn', qc.astype(jnp.float32), kmean)
        gate = jnp.where(jnp.arange(nb) < own, gate, -jnp.inf)
        _, sel = lax.top_k(gate, k_eff)
        logits = []
        for s in range(k_eff):
            ks = kb[bi, hi, sel[..., s]]
            ls = jnp.einsum('bhqd,bhqkd->bhqk', qc, ks, preferred_element_type=jnp.float32) * scale
            logits.append(jnp.where(s < own, ls, -jnp.inf))
        ko = lax.dynamic_index_in_dim(kb, own, axis=2, keepdims=False)
        vo = lax.dynamic_index_in_dim(vb, own, axis=2, keepdims=False)
        lo = jnp.einsum('bhqd,bhkd->bhqk', qc, ko, preferred_element_type=jnp.float32) * scale
        kpos = own * MOBA_BLOCK + jnp.arange(MOBA_BLOCK)
        logits.append(jnp.where(kpos[None, :] <= qpos[:, None], lo, -jnp.inf))
        p = jax.nn.softmax(jnp.concatenate(logits, axis=-1), axis=-1).astype(v.dtype)
        p = p.reshape(B, N_HEADS, Q_CHUNK, k_eff + 1, MOBA_BLOCK)
        out = jnp.einsum('bhqk,bhkd->bhqd', p[..., k_eff, :], vo)
        for s in range(k_eff):
            out = out + jnp.einsum('bhqk,bhqkd->bhqd', p[..., s, :], vb[bi, hi, sel[..., s]])
        return out

    outs = lax.map(chunk, jnp.arange(T // Q_CHUNK))
    return outs.transpose(1, 0, 3, 2, 4).reshape(B, T, W_ATTN)


def paged_rows(pool, layer, page_table, pos, head=None):
    bi = jnp.arange(pos.shape[0]).reshape((-1,) + (1,) * (pos.ndim - 1))
    phys = page_table[bi, pos // PAGE_SIZE]
    off = pos % PAGE_SIZE
    if head is None:
        return pool[layer, phys, off]
    return pool[layer, phys, off, head]


def moba_sample(q, k, v, cache_k, cache_v, page_table, layer):
    DB, S = q.shape[0], q.shape[1]
    nbp = PAST_LEN // MOBA_BLOCK
    own_start = nbp * MOBA_BLOCK
    r = PAST_LEN - own_start
    scale = HEAD_DIM ** -0.5
    qh = q.transpose(0, 2, 1, 3)
    hi = jnp.arange(N_HEADS)[None, :, None, None]
    logits, sel_pos = [], []
    if nbp > 0:
        k_past = cache_k[layer, page_table].reshape(DB, PAST_LEN, N_HEADS, HEAD_DIM)
        kmean = jnp.mean(k_past[:, :own_start].reshape(DB, nbp, MOBA_BLOCK, N_HEADS, HEAD_DIM)
                         .astype(jnp.float32), axis=2)
        gate = jnp.einsum('bhqd,bnhd->bhqn', qh.astype(jnp.float32), kmean)
        _, sel = lax.top_k(gate, min(MOBA_TOPK, nbp))
        for s in range(sel.shape[-1]):
            pos = sel[..., s, None] * MOBA_BLOCK + jnp.arange(MOBA_BLOCK)
            ks = paged_rows(cache_k, layer, page_table, pos, hi)
            logits.append(jnp.einsum('bhqd,bhqkd->bhqk', qh, ks,
                                     preferred_element_type=jnp.float32) * scale)
            sel_pos.append(pos)
    if r > 0:
        own_pos = jnp.broadcast_to(own_start + jnp.arange(r), (DB, r))
        ko = jnp.concatenate([paged_rows(cache_k, layer, page_table, own_pos).astype(k.dtype), k], axis=1)
        vo = jnp.concatenate([paged_rows(cache_v, layer, page_table, own_pos).astype(v.dtype), v], axis=1)
    else:
        ko, vo = k, v
    lo = jnp.einsum('bhqd,bkhd->bhqk', qh, ko, preferred_element_type=jnp.float32) * scale
    allowed = jnp.arange(r + S)[None, :] <= (r + jnp.arange(S))[:, None]
    logits.append(jnp.where(allowed, lo, -jnp.inf))
    n_sel = len(sel_pos)
    p = jax.nn.softmax(jnp.concatenate(logits, axis=-1), axis=-1).astype(v.dtype)
    out = jnp.einsum('bhqk,bkhd->bhqd', p[..., n_sel * MOBA_BLOCK:], vo)
    for s, pos in enumerate(sel_pos):
        vs = paged_rows(cache_v, layer, page_table, pos, hi).astype(v.dtype)
        out = out + jnp.einsum('bhqk,bhqkd->bhqd', p[..., s * MOBA_BLOCK:(s + 1) * MOBA_BLOCK], vs)
    return out.transpose(0, 2, 1, 3).reshape(DB, S, W_ATTN)


def mixer_layer(x, pos0, conv_ctx, pool_ctx, attend, norm_g, w_in, w_out, conv_w, conv_b,
                conv_ln_g, conv_ln_b, pool_w, pool_scale, q_norm_g, k_norm_g):
    N, T, _ = x.shape
    pos = pos0 + jnp.arange(T)
    h = rms_norm(x, norm_g)
    z = jnp.einsum('btd,de->bte', h, w_in)
    sizes = (W_CONV, W_CONV, W_CONV, W_POOL, W_POOL, W_ATTN, W_ATTN, W_ATTN)
    cuts = tuple(int(c) for c in np.cumsum(sizes))
    a, a_glu, g_a, u, g_b, q, k, v, g_c = jnp.split(z, cuts, axis=-1)
    y_a, conv_new = conv_branch(a, a_glu, conv_ctx, conv_w, conv_b, conv_ln_g, conv_ln_b)
    y_b, pool_new = pool_branch(u, pool_ctx, pos0, pool_w, pool_scale)
    q = rope(rms_norm(q.reshape(N, T, N_HEADS, HEAD_DIM), q_norm_g), pos)
    k = rope(rms_norm(k.reshape(N, T, N_HEADS, HEAD_DIM), k_norm_g), pos)
    v = v.reshape(N, T, N_HEADS, HEAD_DIM)
    y_c = attend(q, k, v)
    mix = jnp.concatenate([y_a * jax.nn.silu(g_a), y_b * jax.nn.silu(g_b), y_c * jax.nn.silu(g_c)], axis=-1)
    y = x + jnp.einsum('bte,ed->btd', mix, w_out)
    return y, k, v, conv_new, pool_new


def setup_inputs(seed: int = 0) -> dict:
    key = jax.random.key(seed)
    ks = jax.random.split(key, 20)
    f32 = jnp.float32
    n_pages = PAST_LEN // PAGE_SIZE
    n_used = DEC_BATCH * n_pages
    n_phys = n_used + max(1, n_used // 4)
    page_table = jax.random.permutation(ks[0], n_phys)[:n_used].reshape(DEC_BATCH, n_pages).astype(jnp.int32)
    pool_shape = (DEPTH, n_phys, PAGE_SIZE, N_HEADS, HEAD_DIM)
    return {
        'x_prompt': jax.random.normal(ks[1], (BATCH, SEQ, D_MODEL), f32),
        'x_sample': jax.random.normal(ks[2], (DEC_BATCH, DEC_SEQ, D_MODEL), f32),
        'cache_k': jax.random.normal(ks[3], pool_shape, f32),
        'cache_v': jax.random.normal(ks[4], pool_shape, f32),
        'state_conv': 0.5 * jax.random.normal(ks[5], (DEPTH, DEC_BATCH, CONV_CTX, W_CONV), f32),
        'state_pool': jax.random.normal(ks[6], (DEPTH, DEC_BATCH, POOL_CTX, W_POOL), f32),
        'page_table': page_table,
        'norm_g': 1.0 + 0.02 * jax.random.normal(ks[7], (DEPTH, D_MODEL), f32),
        'w_in': jax.random.normal(ks[8], (DEPTH, D_MODEL, D_IN), f32) * D_MODEL ** -0.5,
        'w_out': jax.random.normal(ks[9], (DEPTH, D_MIX, D_MODEL), f32) * D_MIX ** -0.5,
        'conv_w': jax.random.normal(ks[10], (DEPTH, CONV_WIDTH, W_CONV), f32) * CONV_WIDTH ** -0.5,
        'conv_b': 0.02 * jax.random.normal(ks[11], (DEPTH, W_CONV), f32),
        'conv_ln_g': 1.0 + 0.02 * jax.random.normal(ks[12], (DEPTH, W_CONV), f32),
        'conv_ln_b': 0.02 * jax.random.normal(ks[13], (DEPTH, W_CONV), f32),
        'pool_w': jax.random.normal(ks[14], (DEPTH, len(POOL_WINDOWS), POOL_GROUP, POOL_GROUP), f32) * POOL_GROUP ** -0.5,
        'pool_scale': 1.0 + 0.1 * jax.random.normal(ks[15], (DEPTH, W_POOL), f32),
        'q_norm_g': 1.0 + 0.02 * jax.random.normal(ks[16], (DEPTH, HEAD_DIM), f32),
        'k_norm_g': 1.0 + 0.02 * jax.random.normal(ks[17], (DEPTH, HEAD_DIM), f32),
    }


def reference(x_prompt, x_sample, cache_k, cache_v, state_conv, state_pool, page_table,
              norm_g, w_in, w_out, conv_w, conv_b, conv_ln_g, conv_ln_b, pool_w, pool_scale,
              q_norm_g, k_norm_g):
    hp, hs = x_prompt, x_sample
    kp, vp, cp, pp, ksm, vsm, csm, psm = [], [], [], [], [], [], [], []
    nbatch = x_prompt.shape[0]
    for l in range(DEPTH):
        w = (norm_g[l], w_in[l], w_out[l], conv_w[l], conv_b[l], conv_ln_g[l], conv_ln_b[l],
             pool_w[l], pool_scale[l], q_norm_g[l], k_norm_g[l])
        conv0 = jnp.zeros((nbatch, CONV_CTX, W_CONV), hp.dtype)
        pool0 = jnp.zeros((nbatch, POOL_CTX, W_POOL), hp.dtype)
        hp, k1, v1, c1, p1 = mixer_layer(hp, 0, conv0, pool0, moba_prompt, *w)
        attend_s = functools.partial(moba_sample, cache_k=cache_k, cache_v=cache_v,
                                     page_table=page_table, layer=l)
        hs, k2, v2, c2, p2 = mixer_layer(hs, PAST_LEN, state_conv[l], state_pool[l], attend_s, *w)
        kp.append(k1); vp.append(v1); cp.append(c1); pp.append(p1)
        ksm.append(k2); vsm.append(v2); csm.append(c2); psm.append(p2)
    return (hp, hs, jnp.stack(kp), jnp.stack(vp), jnp.stack(cp), jnp.stack(pp),
            jnp.stack(ksm), jnp.stack(vsm), jnp.stack(csm), jnp.stack(psm))
```

```python
import functools

import jax
import jax.numpy as jnp
from jax import lax
from jax.experimental import pallas as pl
from jax.experimental.pallas import tpu as pltpu

F32 = jnp.float32
BF16 = jnp.bfloat16

D_MODEL = 2048
N_HEADS = 8
HEAD_DIM = 128
W_ATTN = N_HEADS * HEAD_DIM
W_CONV = 512
W_POOL = 512
CONV_WIDTH = 31
CONV_CTX = CONV_WIDTH - 1
POOL_WINDOWS = (2, 4, 8, 16)
POOL_GROUP = W_POOL // len(POOL_WINDOWS)
POOL_CTX = max(POOL_WINDOWS) - 1
MOBA_BLOCK = 256
MOBA_TOPK = 3
PAGE_SIZE = 128
ROPE_THETA = 10000.0
EPS = 1e-6
W_MISC = 3 * W_CONV + 2 * W_POOL
D_IN = W_MISC + 4 * W_ATTN
ATTN_SCALE = HEAD_DIM ** -0.5

COL = 512
N_COL = D_IN // COL
J_Q, J_K, J_V, J_G = 5, 7, 9, 11
CONV_PAD = 32
POOL_PAD = 16
PAGES_PER_STEP = 8
VMEM_LIMIT = 56 * 1024 * 1024

_NT = (((1,), (1,)), ((), ()))


def _silu(x):
    return x * jax.nn.sigmoid(x)


def _norm_rope(z, gain, cos, sin):
    outs = []
    for hh in range(COL // HEAD_DIM):
        xh = z[:, hh * HEAD_DIM:(hh + 1) * HEAD_DIM]
        ms = jnp.mean(xh * xh, axis=-1, keepdims=True)
        y = xh * lax.rsqrt(ms + EPS) * gain
        outs.append(y * cos + pltpu.roll(y, HEAD_DIM // 2, 1) * sin)
    return jnp.concatenate(outs, axis=-1)


def _proj_kernel(x_ref, ng_ref, w_ref, cos_ref, sin_ref, qg_ref, kg_ref,
                 misc_ref, q_ref, k_ref, v_ref, g_ref, h_scr):
    j = pl.program_id(1)

    @pl.when(j == 0)
    def _():
        x = x_ref[...]
        ms = jnp.mean(x * x, axis=-1, keepdims=True)
        h_scr[...] = (x * lax.rsqrt(ms + EPS) * ng_ref[...]).astype(BF16)

    def z():
        return jnp.dot(h_scr[...], w_ref[...], preferred_element_type=F32)

    @pl.when(j < J_Q)
    def _():
        misc_ref[...] = z()

    @pl.when((j >= J_Q) & (j < J_K))
    def _():
        q_ref[...] = _norm_rope(z(), qg_ref[...], cos_ref[...], sin_ref[...])

    @pl.when((j >= J_K) & (j < J_V))
    def _():
        k_ref[...] = _norm_rope(z(), kg_ref[...], cos_ref[...], sin_ref[...])

    @pl.when((j >= J_V) & (j < J_G))
    def _():
        v_ref[...] = z()

    @pl.when(j >= J_G)
    def _():
        g_ref[...] = z()


def _proj(x, norm_g, w_in_bf, cos, sin, qg, kg, tm):
    m = x.shape[0]
    n_pos_tiles = cos.shape[0] // tm

    def sect(j0):
        return lambda i, j: (i, jnp.clip(j - j0, 0, 1))

    out_shape = (jax.ShapeDtypeStruct((m, W_MISC), F32),) + tuple(
        jax.ShapeDtypeStruct((m, W_ATTN), F32) for _ in range(4))
    return pl.pallas_call(
        _proj_kernel,
        out_shape=out_shape,
        grid=(m // tm, N_COL),
        in_specs=[
            pl.BlockSpec((tm, D_MODEL), lambda i, j: (i, 0)),
            pl.BlockSpec((1, D_MODEL), lambda i, j: (0, 0)),
            pl.BlockSpec((D_MODEL, COL), lambda i, j: (0, j)),
            pl.BlockSpec((tm, HEAD_DIM), lambda i, j: (i % n_pos_tiles, 0)),
            pl.BlockSpec((tm, HEAD_DIM), lambda i, j: (i % n_pos_tiles, 0)),
            pl.BlockSpec((1, HEAD_DIM), lambda i, j: (0, 0)),
            pl.BlockSpec((1, HEAD_DIM), lambda i, j: (0, 0)),
        ],
        out_specs=(
            pl.BlockSpec((tm, COL), lambda i, j: (i, jnp.minimum(j, J_Q - 1))),
            pl.BlockSpec((tm, COL), sect(J_Q)),
            pl.BlockSpec((tm, COL), sect(J_K)),
            pl.BlockSpec((tm, COL), sect(J_V)),
            pl.BlockSpec((tm, COL), sect(J_G)),
        ),
        scratch_shapes=[pltpu.VMEM((tm, D_MODEL), BF16)],
        compiler_params=pltpu.CompilerParams(
            dimension_semantics=("parallel", "arbitrary"), vmem_limit_bytes=VMEM_LIMIT),
        name="proj",
    )(x, norm_g, w_in_bf, cos, sin, qg, kg)


def _convpool_kernel(a_ref, b_ref, ga_ref, u_ref, gb_ref, cctx_ref, pctx_ref,
                     cw_ref, cb_ref, lg_ref, lb_ref, pw_ref, ps_ref,
                     mix_ref, cnew_ref, pnew_ref, cs, ps, *, tt, rows, n_t, pos0):
    t = pl.program_id(1)

    @pl.when(t == 0)
    def _():
        cs[0:CONV_PAD - CONV_CTX, :] = jnp.zeros((CONV_PAD - CONV_CTX, W_CONV), F32)
        cs[CONV_PAD - CONV_CTX:CONV_PAD, :] = cctx_ref[0]
        ps[0:POOL_PAD - POOL_CTX, :] = jnp.zeros((POOL_PAD - POOL_CTX, W_POOL), F32)
        ps[POOL_PAD - POOL_CTX:POOL_PAD, :] = pctx_ref[0]

    cs[CONV_PAD:CONV_PAD + tt, :] = a_ref[0] * jax.nn.sigmoid(b_ref[0])
    ps[POOL_PAD:POOL_PAD + tt, :] = u_ref[0]

    for base in range(0, tt, rows):
        acc = jnp.broadcast_to(cb_ref[...], (rows, W_CONV))
        for jj in range(CONV_WIDTH):
            acc = acc + cw_ref[jj:jj + 1, :] * cs[pl.ds(base + CONV_PAD - CONV_CTX + jj, rows), :]
        mu = jnp.mean(acc, axis=-1, keepdims=True)
        d = acc - mu
        var = jnp.mean(d * d, axis=-1, keepdims=True)
        ya = _silu(d * lax.rsqrt(var + EPS) * lg_ref[...] + lb_ref[...])
        mix_ref[0, pl.ds(base, rows), 0:W_CONV] = ya * _silu(ga_ref[0, pl.ds(base, rows), :])
        pos = pos0 + t * tt + base + lax.broadcasted_iota(jnp.int32, (rows, 1), 0)
        for g, w in enumerate(POOL_WINDOWS):
            sl = slice(g * POOL_GROUP, (g + 1) * POOL_GROUP)
            tok = ps[pl.ds(base + POOL_PAD, rows), sl]
            s = tok
            for i in range(1, w):
                s = s + ps[pl.ds(base + POOL_PAD - i, rows), sl]
            cnt = jnp.minimum(w, pos + 1).astype(F32)
            p = s / cnt - tok
            yb = jnp.dot(p.astype(BF16), pw_ref[g], preferred_element_type=F32) * ps_ref[:, sl]
            mix_ref[0, pl.ds(base, rows), W_CONV + g * POOL_GROUP:W_CONV + (g + 1) * POOL_GROUP] = (
                yb * _silu(gb_ref[0, pl.ds(base, rows), sl]))

    @pl.when(t == n_t - 1)
    def _():
        cnew_ref[0] = cs[tt + CONV_PAD - CONV_CTX:tt + CONV_PAD, :]
        pnew_ref[0] = ps[tt + POOL_PAD - POOL_CTX:tt + POOL_PAD, :]

    if n_t > 1:
        @pl.when(t < n_t - 1)
        def _():
            cs[0:CONV_PAD, :] = cs[tt:tt + CONV_PAD, :]
            ps[0:POOL_PAD, :] = ps[tt:tt + POOL_PAD, :]


def _convpool(misc, cctx, pctx, conv_w, conv_b, ln_g, ln_b, pool_w_bf, pool_scale, tt, rows, pos0):
    n, t_len, _ = misc.shape
    n_t = t_len // tt

    def col(c):
        return pl.BlockSpec((1, tt, COL), lambda b, t: (b, t, c))

    def full(shape):
        return pl.BlockSpec(shape, lambda b, t: (0,) * len(shape))

    return pl.pallas_call(
        functools.partial(_convpool_kernel, tt=tt, rows=rows, n_t=n_t, pos0=pos0),
        out_shape=(jax.ShapeDtypeStruct((n, t_len, W_CONV + W_POOL), F32),
                   jax.ShapeDtypeStruct((n, CONV_CTX, W_CONV), F32),
                   jax.ShapeDtypeStruct((n, POOL_CTX, W_POOL), F32)),
        grid=(n, n_t),
        in_specs=[col(0), col(1), col(2), col(3), col(4),
                  pl.BlockSpec((1, CONV_CTX, W_CONV), lambda b, t: (b, 0, 0)),
                  pl.BlockSpec((1, POOL_CTX, W_POOL), lambda b, t: (b, 0, 0)),
                  full((CONV_WIDTH, W_CONV)), full((1, W_CONV)), full((1, W_CONV)), full((1, W_CONV)),
                  full((len(POOL_WINDOWS), POOL_GROUP, POOL_GROUP)), full((1, W_POOL))],
        out_specs=(pl.BlockSpec((1, tt, W_CONV + W_POOL), lambda b, t: (b, t, 0)),
                   pl.BlockSpec((1, CONV_CTX, W_CONV), lambda b, t: (b, 0, 0)),
                   pl.BlockSpec((1, POOL_CTX, W_POOL), lambda b, t: (b, 0, 0))),
        scratch_shapes=[pltpu.VMEM((CONV_PAD + tt, W_CONV), F32),
                        pltpu.VMEM((POOL_PAD + tt, W_POOL), F32)],
        compiler_params=pltpu.CompilerParams(
            dimension_semantics=("parallel", "arbitrary"), vmem_limit_bytes=VMEM_LIMIT),
        name="convpool",
    )(misc, misc, misc, misc, misc, cctx, pctx, conv_w, conv_b, ln_g, ln_b, pool_w_bf, pool_scale)


def _top_blocks(gate, idx, n_valid, axis):
    n_blocks = gate.shape[axis]
    gate = jnp.where(idx < n_valid, gate, -jnp.inf)
    sel = jnp.zeros(gate.shape, F32)
    for _ in range(MOBA_TOPK):
        best = jnp.max(gate, axis=axis, keepdims=True)
        first = jnp.min(jnp.where(gate == best, idx, n_blocks), axis=axis, keepdims=True)
        pick = idx == first
        sel = jnp.where(pick, 1.0, sel)
        gate = jnp.where(pick, -jnp.inf, gate)
    return jnp.where(idx < n_valid, sel, 0.0)


def _moba_kernel(q_ref, k_ref, v_ref, g_ref, o_ref, kbf, vtbf, kmean, sel_scr, *, n_blocks):
    c = pl.program_id(2)

    @pl.when(c == 0)
    def _():
        for n in range(n_blocks):
            kb = k_ref[n * MOBA_BLOCK:(n + 1) * MOBA_BLOCK, :]
            kbf[n] = kb.astype(BF16)
            kmean[n:n + 1, :] = jnp.sum(kb, axis=0, keepdims=True) * (1.0 / MOBA_BLOCK)
            vtbf[n] = v_ref[n * MOBA_BLOCK:(n + 1) * MOBA_BLOCK, :].T.astype(BF16)

    q = q_ref[...]
    gate = lax.dot_general(kmean[...], q, _NT, precision=lax.Precision.HIGHEST,
                           preferred_element_type=F32)
    blk = lax.broadcasted_iota(jnp.int32, gate.shape, 0)
    sel_scr[...] = _top_blocks(gate, blk, c, 0)

    qs = (q * ATTN_SCALE).astype(BF16)
    s = lax.dot_general(kbf[c], qs, _NT, preferred_element_type=F32)
    kpos = lax.broadcasted_iota(jnp.int32, s.shape, 0)
    qpos = lax.broadcasted_iota(jnp.int32, s.shape, 1)
    s = jnp.where(kpos <= qpos, s, -jnp.inf)
    m0 = jnp.max(s, axis=0, keepdims=True)
    p = jnp.exp(s - m0)
    l0 = jnp.sum(p, axis=0, keepdims=True)
    acc0 = jnp.dot(vtbf[c], p.astype(BF16), preferred_element_type=F32)

    def past(n, carry):
        m, l, acc = carry
        s = lax.dot_general(kbf[n], qs, _NT, preferred_element_type=F32)
        s = jnp.where(sel_scr[pl.ds(n, 1), :] > 0.0, s, -jnp.inf)
        m_new = jnp.maximum(m, jnp.max(s, axis=0, keepdims=True))
        alpha = jnp.exp(m - m_new)
        p = jnp.exp(s - m_new)
        l = alpha * l + jnp.sum(p, axis=0, keepdims=True)
        acc = alpha * acc + jnp.dot(vtbf[n], p.astype(BF16), preferred_element_type=F32)
        return m_new, l, acc

    _, l, acc = lax.fori_loop(0, c, past, (m0, l0, acc0))
    o_ref[...] = (acc / l).T * _silu(g_ref[...])


def _moba(q, k, v, g, batch, t_len):
    m = q.shape[0]
    n_blocks = t_len // MOBA_BLOCK
    qspec = pl.BlockSpec((MOBA_BLOCK, HEAD_DIM), lambda b, h, c: (b * n_blocks + c, h))
    kvspec = pl.BlockSpec((t_len, HEAD_DIM), lambda b, h, c: (b, h))
    return pl.pallas_call(
        functools.partial(_moba_kernel, n_blocks=n_blocks),
        out_shape=jax.ShapeDtypeStruct((m, W_ATTN), F32),
        grid=(batch, N_HEADS, n_blocks),
        in_specs=[qspec, kvspec, kvspec, qspec],
        out_specs=qspec,
        scratch_shapes=[pltpu.VMEM((n_blocks, MOBA_BLOCK, HEAD_DIM), BF16),
                        pltpu.VMEM((n_blocks, HEAD_DIM, MOBA_BLOCK), BF16),
                        pltpu.VMEM((n_blocks, HEAD_DIM), F32),
                        pltpu.VMEM((n_blocks, MOBA_BLOCK), F32)],
        compiler_params=pltpu.CompilerParams(
            dimension_semantics=("parallel", "parallel", "arbitrary"), vmem_limit_bytes=VMEM_LIMIT),
        name="moba",
    )(q, k, v, g)


def _head_rows(x4):
    rows = jnp.concatenate(
        [jnp.broadcast_to(x4[s:s + 1, :], (N_HEADS, W_ATTN)) for s in range(x4.shape[0])], axis=0)
    row_h = lax.broadcasted_iota(jnp.int32, rows.shape, 0) % N_HEADS
    lane_h = lax.broadcasted_iota(jnp.int32, rows.shape, 1) // HEAD_DIM
    return jnp.where(row_h == lane_h, rows, 0.0)


def _page_rows(page_ref):
    return jnp.concatenate([page_ref[:, h, :] for h in range(N_HEADS)], axis=-1)


def _dec_qk_kernel(pt_ref, *refs, n_steps, dec_seq):
    pages = refs[:PAGES_PER_STEP]
    q_ref, kn_ref, vn_ref, p_ref, own_ref, kmean, wq = refs[PAGES_PER_STEP:]
    st = pl.program_id(1)
    n_rows = dec_seq * N_HEADS
    n_pages = n_steps * PAGES_PER_STEP
    pages_per_block = MOBA_BLOCK // PAGE_SIZE
    n_blocks = n_pages // pages_per_block

    @pl.when(st == 0)
    def _():
        wq[...] = _head_rows(q_ref[0])

    w = wq[...]
    for i in range(PAGES_PER_STEP):
        page = _page_rows(pages[i])
        colsum = jnp.sum(page, axis=0, keepdims=True)
        if i % pages_per_block == 0:
            blocksum = colsum
        else:
            blocksum = blocksum + colsum
        if i % pages_per_block == pages_per_block - 1:
            kmean[pl.ds(st * (PAGES_PER_STEP // pages_per_block) + i // pages_per_block, 1), :] = (
                blocksum * (1.0 / MOBA_BLOCK))
        p_ref[0, st * PAGES_PER_STEP + i] = (
            lax.dot_general(w, page, _NT, preferred_element_type=F32) * ATTN_SCALE)

    @pl.when(st == n_steps - 1)
    def _():
        gate = lax.dot_general(w, kmean[...], _NT, precision=lax.Precision.HIGHEST,
                               preferred_element_type=F32)
        blk = lax.broadcasted_iota(jnp.int32, gate.shape, 1)
        sel = _top_blocks(gate, blk, n_blocks, 1)
        lo = lax.dot_general(w, kn_ref[0], _NT, preferred_element_type=F32) * ATTN_SCALE
        tok = lax.broadcasted_iota(jnp.int32, lo.shape, 0) // N_HEADS
        key = lax.broadcasted_iota(jnp.int32, lo.shape, 1)
        lo = jnp.where(key <= tok, lo, -jnp.inf)
        m = jnp.max(lo, axis=1, keepdims=True)
        mp = jnp.full((n_rows, PAGE_SIZE), -jnp.inf, F32)
        for pg in range(n_pages):
            keep = sel[:, pg // pages_per_block:pg // pages_per_block + 1] > 0.0
            mp = jnp.maximum(mp, jnp.where(keep, p_ref[0, pg], -jnp.inf))
        m = jnp.maximum(m, jnp.max(mp, axis=1, keepdims=True))
        p_own = jnp.exp(lo - m)
        lsum = jnp.zeros((n_rows, PAGE_SIZE), F32)
        for pg in range(n_pages):
            keep = sel[:, pg // pages_per_block:pg // pages_per_block + 1] > 0.0
            e = jnp.where(keep, jnp.exp(p_ref[0, pg] - m), 0.0)
            p_ref[0, pg] = e
            lsum = lsum + e
        inv = 1.0 / (jnp.sum(lsum, axis=1, keepdims=True) + jnp.sum(p_own, axis=1, keepdims=True))
        for pg in range(n_pages):
            p_ref[0, pg] = p_ref[0, pg] * inv
        p_own = p_own * inv
        vn = vn_ref[0]
        own = p_own[:, 0:1] * vn[0:1, :]
        for jj in range(1, dec_seq):
            own = own + p_own[:, jj:jj + 1] * vn[jj:jj + 1, :]
        own_ref[0] = own


def _page_spec(layer, i, n_pages):
    return pl.BlockSpec(
        (None, None, PAGE_SIZE, N_HEADS, HEAD_DIM),
        lambda b, st, pt: (layer, pt[b * n_pages + st * PAGES_PER_STEP + i], 0, 0, 0))


def _dec_qk(pt_flat, cache_k, layer, q, kn, vn):
    db, dec_seq, _ = q.shape
    n_pages = pt_flat.shape[0] // db
    n_steps = n_pages // PAGES_PER_STEP
    n_rows = dec_seq * N_HEADS
    tok = pl.BlockSpec((1, dec_seq, W_ATTN), lambda b, st, pt: (b, 0, 0))
    grid_spec = pltpu.PrefetchScalarGridSpec(
        num_scalar_prefetch=1,
        grid=(db, n_steps),
        in_specs=[_page_spec(layer, i, n_pages) for i in range(PAGES_PER_STEP)] + [tok, tok, tok],
        out_specs=(pl.BlockSpec((1, n_pages, n_rows, PAGE_SIZE), lambda b, st, pt: (b, 0, 0, 0)),
                   pl.BlockSpec((1, n_rows, W_ATTN), lambda b, st, pt: (b, 0, 0))),
        scratch_shapes=[pltpu.VMEM((n_pages * PAGE_SIZE // MOBA_BLOCK, W_ATTN), F32),
                        pltpu.VMEM((n_rows, W_ATTN), F32)],
    )
    return pl.pallas_call(
        functools.partial(_dec_qk_kernel, n_steps=n_steps, dec_seq=dec_seq),
        out_shape=(jax.ShapeDtypeStruct((db, n_pages, n_rows, PAGE_SIZE), F32),
                   jax.ShapeDtypeStruct((db, n_rows, W_ATTN), F32)),
        grid_spec=grid_spec,
        compiler_params=pltpu.CompilerParams(
            dimension_semantics=("parallel", "arbitrary"), vmem_limit_bytes=VMEM_LIMIT),
        name="dec_qk",
    )(pt_flat, *([cache_k] * PAGES_PER_STEP), q, kn, vn)


def _dec_pv_kernel(pt_ref, *refs, n_steps, dec_seq):
    pages = refs[:PAGES_PER_STEP]
    p_ref, own_ref, g_ref, o_ref, acc = refs[PAGES_PER_STEP:]
    st = pl.program_id(1)

    @pl.when(st == 0)
    def _():
        acc[...] = own_ref[0]

    a = acc[...]
    for i in range(PAGES_PER_STEP):
        a = a + jnp.dot(p_ref[0, st * PAGES_PER_STEP + i], _page_rows(pages[i]), preferred_element_type=F32)
    acc[...] = a

    @pl.when(st == n_steps - 1)
    def _():
        row_h = lax.broadcasted_iota(jnp.int32, a.shape, 0) % N_HEADS
        lane_h = lax.broadcasted_iota(jnp.int32, a.shape, 1) // HEAD_DIM
        diag = jnp.where(row_h == lane_h, a, 0.0)
        heads = jnp.concatenate(
            [jnp.sum(diag[s * N_HEADS:(s + 1) * N_HEADS], axis=0, keepdims=True) for s in range(dec_seq)], axis=0)
        o_ref[0] = heads * _silu(g_ref[0])


def _dec_pv(pt_flat, cache_v, layer, probs, own, g):
    db, n_pages, n_rows, _ = probs.shape
    dec_seq = n_rows // N_HEADS
    n_steps = n_pages // PAGES_PER_STEP
    tok = pl.BlockSpec((1, dec_seq, W_ATTN), lambda b, st, pt: (b, 0, 0))
    grid_spec = pltpu.PrefetchScalarGridSpec(
        num_scalar_prefetch=1,
        grid=(db, n_steps),
        in_specs=[_page_spec(layer, i, n_pages) for i in range(PAGES_PER_STEP)] + [
            pl.BlockSpec((1, n_pages, n_rows, PAGE_SIZE), lambda b, st, pt: (b, 0, 0, 0)),
            pl.BlockSpec((1, n_rows, W_ATTN), lambda b, st, pt: (b, 0, 0)),
            tok],
        out_specs=tok,
        scratch_shapes=[pltpu.VMEM((n_rows, W_ATTN), F32)],
    )
    return pl.pallas_call(
        functools.partial(_dec_pv_kernel, n_steps=n_steps, dec_seq=dec_seq),
        out_shape=jax.ShapeDtypeStruct((db, dec_seq, W_ATTN), F32),
        grid_spec=grid_spec,
        compiler_params=pltpu.CompilerParams(
            dimension_semantics=("parallel", "arbitrary"), vmem_limit_bytes=VMEM_LIMIT),
        name="dec_pv",
    )(pt_flat, *([cache_v] * PAGES_PER_STEP), probs, own, g)


def _outproj_kernel(ab_ref, c_ref, x_ref, wa_ref, wc_ref, y_ref, mix_scr):
    j = pl.program_id(1)

    @pl.when(j == 0)
    def _():
        mix_scr[:, 0:W_CONV + W_POOL] = ab_ref[...].astype(BF16)
        mix_scr[:, W_CONV + W_POOL:] = c_ref[...].astype(BF16)

    y = jnp.dot(mix_scr[:, 0:W_CONV + W_POOL], wa_ref[...], preferred_element_type=F32)
    y = y + jnp.dot(mix_scr[:, W_CONV + W_POOL:], wc_ref[...], preferred_element_type=F32)
    y_ref[...] = x_ref[...] + y


def _outproj(mix_ab, mix_c, x, w_out_bf, tm):
    m = x.shape[0]
    half = W_CONV + W_POOL
    return pl.pallas_call(
        _outproj_kernel,
        out_shape=jax.ShapeDtypeStruct((m, D_MODEL), F32),
        grid=(m // tm, D_MODEL // COL),
        in_specs=[pl.BlockSpec((tm, half), lambda i, j: (i, 0)),
                  pl.BlockSpec((tm, W_ATTN), lambda i, j: (i, 0)),
                  pl.BlockSpec((tm, COL), lambda i, j: (i, j)),
                  pl.BlockSpec((half, COL), lambda i, j: (0, j)),
                  pl.BlockSpec((W_ATTN, COL), lambda i, j: (1, j))],
        out_specs=pl.BlockSpec((tm, COL), lambda i, j: (i, j)),
        scratch_shapes=[pltpu.VMEM((tm, half + W_ATTN), BF16)],
        compiler_params=pltpu.CompilerParams(
            dimension_semantics=("parallel", "arbitrary"), vmem_limit_bytes=VMEM_LIMIT),
        name="outproj",
    )(mix_ab, mix_c, x, w_out_bf, w_out_bf)


def _rope_tables(pos):
    half = HEAD_DIM // 2
    inv = ROPE_THETA ** (-jnp.arange(half, dtype=F32) / half)
    ang = pos.astype(F32)[:, None] * inv[None, :]
    cos, sin = jnp.cos(ang), jnp.sin(ang)
    return jnp.concatenate([cos, cos], axis=-1), jnp.concatenate([-sin, sin], axis=-1)


def kernel(x_prompt, x_sample, cache_k, cache_v, state_conv, state_pool, page_table, norm_g, w_in, w_out,
           conv_w, conv_b, conv_ln_g, conv_ln_b, pool_w, pool_scale, q_norm_g, k_norm_g):
    batch, seq, _ = x_prompt.shape
    db, dec_seq, _ = x_sample.shape
    depth = w_in.shape[0]
    past_len = page_table.shape[1] * PAGE_SIZE

    hp = x_prompt.reshape(batch * seq, D_MODEL)
    hs = x_sample.reshape(db * dec_seq, D_MODEL)
    cos_p, sin_p = _rope_tables(jnp.arange(seq, dtype=jnp.int32))
    cos_s, sin_s = _rope_tables(past_len + jnp.arange(db * dec_seq, dtype=jnp.int32) % dec_seq)
    pt_flat = page_table.reshape(-1)
    zero_cctx = jnp.zeros((batch, CONV_CTX, W_CONV), F32)
    zero_pctx = jnp.zeros((batch, POOL_CTX, W_POOL), F32)

    outs = [[] for _ in range(8)]
    for l in range(depth):
        w_in_bf = w_in[l].astype(BF16)
        w_out_bf = w_out[l].astype(BF16)
        pool_w_bf = pool_w[l].astype(BF16)
        small = (conv_w[l], conv_b[l][None], conv_ln_g[l][None], conv_ln_b[l][None], pool_w_bf, pool_scale[l][None])
        ng, qg, kg = norm_g[l][None], q_norm_g[l][None], k_norm_g[l][None]

        misc, q, k, v, g = _proj(hp, ng, w_in_bf, cos_p, sin_p, qg, kg, tm=512)
        mix_ab, c_new, p_new = _convpool(misc.reshape(batch, seq, W_MISC), zero_cctx, zero_pctx, *small,
                                         tt=256, rows=64, pos0=0)
        mix_c = _moba(q, k, v, g, batch, seq)
        hp = _outproj(mix_ab.reshape(batch * seq, -1), mix_c, hp, w_out_bf, tm=512)
        outs[0].append(k.reshape(batch, seq, N_HEADS, HEAD_DIM))
        outs[1].append(v.reshape(batch, seq, N_HEADS, HEAD_DIM))
        outs[2].append(c_new)
        outs[3].append(p_new)

        misc, q, k, v, g = _proj(hs, ng, w_in_bf, cos_s, sin_s, qg, kg, tm=db * dec_seq)
        mix_ab, c_new, p_new = _convpool(misc.reshape(db, dec_seq, W_MISC), state_conv[l], state_pool[l], *small,
                                         tt=dec_seq, rows=dec_seq, pos0=past_len)
        q3, k3, v3, g3 = (a.reshape(db, dec_seq, W_ATTN) for a in (q, k, v, g))
        probs, own = _dec_qk(pt_flat, cache_k, l, q3, k3, v3)
        mix_c = _dec_pv(pt_flat, cache_v, l, probs, own, g3)
        hs = _outproj(mix_ab.reshape(db * dec_seq, -1), mix_c.reshape(db * dec_seq, W_ATTN), hs, w_out_bf,
                      tm=db * dec_seq)
        outs[4].append(k3.reshape(db, dec_seq, N_HEADS, HEAD_DIM))
        outs[5].append(v3.reshape(db, dec_seq, N_HEADS, HEAD_DIM))
        outs[6].append(c_new)
        outs[7].append(p_new)

    return (hp.reshape(batch, seq, D_MODEL), hs.reshape(db, dec_seq, D_MODEL),
            *(jnp.stack(o) for o in outs))
```

```python
import functools
import math

import jax
import jax.numpy as jnp
from jax import lax
from jax.experimental import pallas as pl
from jax.experimental.pallas import tpu as pltpu

F32 = jnp.float32
BF16 = jnp.bfloat16

D_MODEL = 2048
N_HEADS = 8
HEAD_DIM = 128
W_ATTN = N_HEADS * HEAD_DIM
W_CONV = 512
W_POOL = 512
CONV_WIDTH = 31
CONV_CTX = CONV_WIDTH - 1
POOL_WINDOWS = (2, 4, 8, 16)
POOL_GROUP = W_POOL // len(POOL_WINDOWS)
POOL_CTX = max(POOL_WINDOWS) - 1
MOBA_BLOCK = 256
MOBA_TOPK = 3
PAGE_SIZE = 128
ROPE_THETA = 10000.0
EPS = 1e-6
W_MISC = 3 * W_CONV + 2 * W_POOL
D_IN = W_MISC + 4 * W_ATTN
ATTN_SCALE = HEAD_DIM ** -0.5
LOG2E = math.log2(math.e)

COL = 512
COL_HEADS = COL // HEAD_DIM
N_COL = D_IN // COL
J_Q, J_K, J_V, J_G = 5, 7, 9, 11
CONV_PAD = 32
POOL_PAD = 16
MOBA_HEADS = 4
PAGES_PER_STEP = 8
VMEM_LIMIT = 56 * 1024 * 1024

_NT = (((1,), (1,)), ((), ()))


def _silu(x):
    return x * jax.nn.sigmoid(x)


def _norm_rope(z, gain, cos, sin):
    outs = []
    for hh in range(COL_HEADS):
        xh = z[:, hh * HEAD_DIM:(hh + 1) * HEAD_DIM]
        ms = jnp.mean(xh * xh, axis=-1, keepdims=True)
        y = xh * lax.rsqrt(ms + EPS) * gain
        outs.append(y * cos + pltpu.roll(y, HEAD_DIM // 2, 1) * sin)
    return jnp.concatenate(outs, axis=-1)


def _store_heads(fin_ref, tile, first_head):
    tm = tile.shape[0]
    for hh in range(COL_HEADS):
        fin_ref[pl.ds(first_head + hh, tm, stride=N_HEADS), :] = tile[:, hh * HEAD_DIM:(hh + 1) * HEAD_DIM]


def _proj_kernel(*refs, aliased):
    x_ref, ng_ref, w_ref, cos_ref, sin_ref, qg_ref, kg_ref = refs[:7]
    misc_ref, q_ref, k_ref, v_ref, g_ref, kfin_ref, vfin_ref, h_scr = refs[7 + 2 * aliased:]
    j = pl.program_id(1)

    @pl.when(j == 0)
    def _():
        x = x_ref[...]
        ms = jnp.mean(x * x, axis=-1, keepdims=True)
        h_scr[...] = (x * lax.rsqrt(ms + EPS) * ng_ref[...]).astype(BF16)

    def z():
        return jnp.dot(h_scr[...], w_ref[...], preferred_element_type=F32)

    @pl.when(j < J_Q)
    def _():
        misc_ref[...] = z()

    @pl.when((j >= J_Q) & (j < J_K))
    def _():
        q_ref[...] = _norm_rope(z(), qg_ref[...], cos_ref[...], sin_ref[...])

    @pl.when((j >= J_K) & (j < J_V))
    def _():
        k = _norm_rope(z(), kg_ref[...], cos_ref[...], sin_ref[...])
        k_ref[...] = k
        _store_heads(kfin_ref, k, (j - J_K) * COL_HEADS)

    @pl.when((j >= J_V) & (j < J_G))
    def _():
        v = z()
        v_ref[...] = v
        _store_heads(vfin_ref, v, (j - J_V) * COL_HEADS)

    @pl.when(j >= J_G)
    def _():
        g_ref[...] = z()


def _proj(x, norm_g, w_in_bf, cos, sin, qg, kg, tm, layer=0, depth=1, kv_fin=None):
    m = x.shape[0]
    n_pos_tiles = cos.shape[0] // tm
    aliased = kv_fin is not None

    def sect(j0):
        return lambda i, j: (i, jnp.clip(j - j0, 0, 1))

    fin_shape = jax.ShapeDtypeStruct((depth, m * N_HEADS, HEAD_DIM), F32)
    fin_spec = pl.BlockSpec((None, tm * N_HEADS, HEAD_DIM), lambda i, j: (layer, i, 0))
    out_shape = (jax.ShapeDtypeStruct((m, W_MISC), F32),) + tuple(
        jax.ShapeDtypeStruct((m, W_ATTN), F32) for _ in range(4)) + (fin_shape, fin_shape)
    in_specs = [
        pl.BlockSpec((tm, D_MODEL), lambda i, j: (i, 0)),
        pl.BlockSpec((1, D_MODEL), lambda i, j: (0, 0)),
        pl.BlockSpec((D_MODEL, COL), lambda i, j: (0, j)),
        pl.BlockSpec((tm, HEAD_DIM), lambda i, j: (i % n_pos_tiles, 0)),
        pl.BlockSpec((tm, HEAD_DIM), lambda i, j: (i % n_pos_tiles, 0)),
        pl.BlockSpec((1, HEAD_DIM), lambda i, j: (0, 0)),
        pl.BlockSpec((1, HEAD_DIM), lambda i, j: (0, 0)),
    ]
    args = [x, norm_g, w_in_bf, cos, sin, qg, kg]
    aliases = {}
    if aliased:
        in_specs += [pl.BlockSpec(memory_space=pl.ANY), pl.BlockSpec(memory_space=pl.ANY)]
        args += list(kv_fin)
        aliases = {7: 5, 8: 6}
    return pl.pallas_call(
        functools.partial(_proj_kernel, aliased=aliased),
        out_shape=out_shape,
        grid=(m // tm, N_COL),
        in_specs=in_specs,
        out_specs=(
            pl.BlockSpec((tm, COL), lambda i, j: (i, jnp.minimum(j, J_Q - 1))),
            pl.BlockSpec((tm, COL), sect(J_Q)),
            pl.BlockSpec((tm, COL), sect(J_K)),
            pl.BlockSpec((tm, COL), sect(J_V)),
            pl.BlockSpec((tm, COL), sect(J_G)),
            fin_spec, fin_spec,
        ),
        scratch_shapes=[pltpu.VMEM((tm, D_MODEL), BF16)],
        input_output_aliases=aliases,
        compiler_params=pltpu.CompilerParams(
            dimension_semantics=("parallel", "arbitrary"), vmem_limit_bytes=VMEM_LIMIT),
        name="proj",
    )(*args)


def _convpool_kernel(a_ref, b_ref, ga_ref, u_ref, gb_ref, cctx_ref, pctx_ref,
                     cw_ref, cb_ref, lg_ref, lb_ref, pw_ref, ps_ref,
                     mix_ref, cnew_ref, pnew_ref, cs, ps, *, tt, rows, n_t, pos0):
    t = pl.program_id(1)

    @pl.when(t == 0)
    def _():
        cs[0:CONV_PAD - CONV_CTX, :] = jnp.zeros((CONV_PAD - CONV_CTX, W_CONV), F32)
        cs[CONV_PAD - CONV_CTX:CONV_PAD, :] = cctx_ref[0]
        ps[0:POOL_PAD - POOL_CTX, :] = jnp.zeros((POOL_PAD - POOL_CTX, W_POOL), F32)
        ps[POOL_PAD - POOL_CTX:POOL_PAD, :] = pctx_ref[0]

    cs[CONV_PAD:CONV_PAD + tt, :] = a_ref[0] * jax.nn.sigmoid(b_ref[0])
    ps[POOL_PAD:POOL_PAD + tt, :] = u_ref[0]

    for base in range(0, tt, rows):
        acc = jnp.broadcast_to(cb_ref[...], (rows, W_CONV))
        for jj in range(CONV_WIDTH):
            acc = acc + cw_ref[jj:jj + 1, :] * cs[pl.ds(base + CONV_PAD - CONV_CTX + jj, rows), :]
        mu = jnp.mean(acc, axis=-1, keepdims=True)
        d = acc - mu
        var = jnp.mean(d * d, axis=-1, keepdims=True)
        ya = _silu(d * lax.rsqrt(var + EPS) * lg_ref[...] + lb_ref[...])
        mix_ref[0, pl.ds(base, rows), 0:W_CONV] = ya * _silu(ga_ref[0, pl.ds(base, rows), :])
        pos = pos0 + t * tt + base + lax.broadcasted_iota(jnp.int32, (rows, 1), 0)
        for g, w in enumerate(POOL_WINDOWS):
            sl = slice(g * POOL_GROUP, (g + 1) * POOL_GROUP)
            tok = ps[pl.ds(base + POOL_PAD, rows), sl]
            s = tok
            for i in range(1, w):
                s = s + ps[pl.ds(base + POOL_PAD - i, rows), sl]
            cnt = jnp.minimum(w, pos + 1).astype(F32)
            p = s / cnt - tok
            yb = jnp.dot(p.astype(BF16), pw_ref[g], preferred_element_type=F32) * ps_ref[:, sl]
            mix_ref[0, pl.ds(base, rows), W_CONV + g * POOL_GROUP:W_CONV + (g + 1) * POOL_GROUP] = (
                yb * _silu(gb_ref[0, pl.ds(base, rows), sl]))

    @pl.when(t == n_t - 1)
    def _():
        cnew_ref[0] = cs[tt + CONV_PAD - CONV_CTX:tt + CONV_PAD, :]
        pnew_ref[0] = ps[tt + POOL_PAD - POOL_CTX:tt + POOL_PAD, :]

    if n_t > 1:
        @pl.when(t < n_t - 1)
        def _():
            cs[0:CONV_PAD, :] = cs[tt:tt + CONV_PAD, :]
            ps[0:POOL_PAD, :] = ps[tt:tt + POOL_PAD, :]


def _convpool(misc, cctx, pctx, conv_w, conv_b, ln_g, ln_b, pool_w_bf, pool_scale, tt, rows, pos0):
    n, t_len, _ = misc.shape
    n_t = t_len // tt

    def col(c):
        return pl.BlockSpec((1, tt, COL), lambda b, t: (b, t, c))

    def full(shape):
        return pl.BlockSpec(shape, lambda b, t: (0,) * len(shape))

    return pl.pallas_call(
        functools.partial(_convpool_kernel, tt=tt, rows=rows, n_t=n_t, pos0=pos0),
        out_shape=(jax.ShapeDtypeStruct((n, t_len, W_CONV + W_POOL), F32),
                   jax.ShapeDtypeStruct((n, CONV_CTX, W_CONV), F32),
                   jax.ShapeDtypeStruct((n, POOL_CTX, W_POOL), F32)),
        grid=(n, n_t),
        in_specs=[col(0), col(1), col(2), col(3), col(4),
                  pl.BlockSpec((1, CONV_CTX, W_CONV), lambda b, t: (b, 0, 0)),
                  pl.BlockSpec((1, POOL_CTX, W_POOL), lambda b, t: (b, 0, 0)),
                  full((CONV_WIDTH, W_CONV)), full((1, W_CONV)), full((1, W_CONV)), full((1, W_CONV)),
                  full((len(POOL_WINDOWS), POOL_GROUP, POOL_GROUP)), full((1, W_POOL))],
        out_specs=(pl.BlockSpec((1, tt, W_CONV + W_POOL), lambda b, t: (b, t, 0)),
                   pl.BlockSpec((1, CONV_CTX, W_CONV), lambda b, t: (b, 0, 0)),
                   pl.BlockSpec((1, POOL_CTX, W_POOL), lambda b, t: (b, 0, 0))),
        scratch_shapes=[pltpu.VMEM((CONV_PAD + tt, W_CONV), F32),
                        pltpu.VMEM((POOL_PAD + tt, W_POOL), F32)],
        compiler_params=pltpu.CompilerParams(
            dimension_semantics=("parallel", "arbitrary"), vmem_limit_bytes=VMEM_LIMIT),
        name="convpool",
    )(misc, misc, misc, misc, misc, cctx, pctx, conv_w, conv_b, ln_g, ln_b, pool_w_bf, pool_scale)


def _top_blocks(gate, idx, n_valid, axis):
    n_blocks = gate.shape[axis]
    gate = jnp.where(idx < n_valid, gate, -jnp.inf)
    sel = jnp.zeros(gate.shape, F32)
    for _ in range(MOBA_TOPK):
        best = jnp.max(gate, axis=axis, keepdims=True)
        first = jnp.min(jnp.where(gate == best, idx, n_blocks), axis=axis, keepdims=True)
        pick = idx == first
        sel = jnp.where(pick, 1.0, sel)
        gate = jnp.where(pick, -jnp.inf, gate)
    return jnp.where(idx < n_valid, sel, 0.0)


def _moba_kernel(q_ref, k_ref, v_ref, g_ref, o_ref, kbf, vtbf, kmean, sel_scr, acc_scr, qs_scr, s_even, s_odd,
                 *, n_blocks):
    c = pl.program_id(2)

    @pl.when(c == 0)
    def _():
        def prep(n, carry):
            rows = pl.ds(pl.multiple_of(n * MOBA_BLOCK, MOBA_BLOCK), MOBA_BLOCK)
            for hh in range(MOBA_HEADS):
                lanes = slice(hh * HEAD_DIM, (hh + 1) * HEAD_DIM)
                kb = k_ref[rows, lanes]
                kbf[hh, n] = kb.astype(BF16)
                kmean[hh, pl.ds(n, 1), :] = jnp.sum(kb, axis=0, keepdims=True) * (1.0 / MOBA_BLOCK)
                vtbf[hh, n] = v_ref[rows, lanes].T.astype(BF16)
            return carry
        lax.fori_loop(0, n_blocks, prep, 0)

    qf = [q_ref[:, hh * HEAD_DIM:(hh + 1) * HEAD_DIM] for hh in range(MOBA_HEADS)]
    for hh in range(MOBA_HEADS):
        qs_scr[hh] = (qf[hh] * (ATTN_SCALE * LOG2E)).astype(BF16)
    scores = [lax.dot_general(kbf[hh, c], qs_scr[hh], _NT, preferred_element_type=F32)
              for hh in range(MOBA_HEADS)]
    gates = [lax.dot_general(kmean[hh], qf[hh], _NT, precision=lax.Precision.HIGHEST,
                             preferred_element_type=F32) for hh in range(MOBA_HEADS)]
    kpos = lax.broadcasted_iota(jnp.int32, scores[0].shape, 0)
    qpos = lax.broadcasted_iota(jnp.int32, scores[0].shape, 1)
    blk = lax.broadcasted_iota(jnp.int32, gates[0].shape, 0)
    stats = []
    for hh in range(MOBA_HEADS):
        s = jnp.where(kpos <= qpos, scores[hh], -jnp.inf)
        m0 = jnp.max(s, axis=0, keepdims=True)
        p = jnp.exp2(s - m0)
        stats.append((m0, jnp.sum(p, axis=0, keepdims=True)))
        acc_scr[hh] = jnp.dot(vtbf[hh, c], p.astype(BF16), preferred_element_type=F32)
    for hh in range(MOBA_HEADS):
        sel_scr[hh] = _top_blocks(gates[hh], blk, c, 0)

    def score(buf, n):
        for hh in range(MOBA_HEADS):
            buf[hh] = lax.dot_general(kbf[hh, n], qs_scr[hh], _NT, preferred_element_type=F32)

    def reduce(buf, n, carry):
        new = []
        for hh in range(MOBA_HEADS):
            m, l = carry[hh]
            s = jnp.where(sel_scr[hh, pl.ds(n, 1), :] > 0.0, buf[hh], -jnp.inf)
            m_new = jnp.maximum(m, jnp.max(s, axis=0, keepdims=True))
            alpha = jnp.exp2(m - m_new)
            p = jnp.exp2(s - m_new)
            acc_scr[hh] = alpha * acc_scr[hh] + jnp.dot(vtbf[hh, n], p.astype(BF16), preferred_element_type=F32)
            new.append((m_new, alpha * l + jnp.sum(p, axis=0, keepdims=True)))
        return tuple(new)

    def past(i, carry):
        n = 2 * i
        score(s_odd, n + 1)
        carry = reduce(s_even, n, carry)
        score(s_even, jnp.minimum(n + 2, n_blocks - 1))
        return reduce(s_odd, n + 1, carry)

    score(s_even, 0)
    stats = lax.fori_loop(0, (c + 1) // 2, past, tuple(stats))
    for hh in range(MOBA_HEADS):
        lanes = slice(hh * HEAD_DIM, (hh + 1) * HEAD_DIM)
        o_ref[:, lanes] = (acc_scr[hh] / stats[hh][1]).T * _silu(g_ref[:, lanes])


def _moba(q, k, v, g, batch, t_len):
    m = q.shape[0]
    n_blocks = t_len // MOBA_BLOCK
    width = MOBA_HEADS * HEAD_DIM
    qspec = pl.BlockSpec((MOBA_BLOCK, width), lambda b, h, c: (b * n_blocks + c, h))
    kvspec = pl.BlockSpec((t_len, width), lambda b, h, c: (b, h))
    return pl.pallas_call(
        functools.partial(_moba_kernel, n_blocks=n_blocks),
        out_shape=jax.ShapeDtypeStruct((m, W_ATTN), F32),
        grid=(batch, N_HEADS // MOBA_HEADS, n_blocks),
        in_specs=[qspec, kvspec, kvspec, qspec],
        out_specs=qspec,
        scratch_shapes=[pltpu.VMEM((MOBA_HEADS, n_blocks, MOBA_BLOCK, HEAD_DIM), BF16),
                        pltpu.VMEM((MOBA_HEADS, n_blocks, HEAD_DIM, MOBA_BLOCK), BF16),
                        pltpu.VMEM((MOBA_HEADS, n_blocks, HEAD_DIM), F32),
                        pltpu.VMEM((MOBA_HEADS, n_blocks, MOBA_BLOCK), F32),
                        pltpu.VMEM((MOBA_HEADS, HEAD_DIM, MOBA_BLOCK), F32),
                        pltpu.VMEM((MOBA_HEADS, MOBA_BLOCK, HEAD_DIM), BF16),
                        pltpu.VMEM((MOBA_HEADS, MOBA_BLOCK, MOBA_BLOCK), F32),
                        pltpu.VMEM((MOBA_HEADS, MOBA_BLOCK, MOBA_BLOCK), F32)],
        compiler_params=pltpu.CompilerParams(
            dimension_semantics=("parallel", "parallel", "arbitrary"), vmem_limit_bytes=VMEM_LIMIT),
        name="moba",
    )(q, k, v, g)


def _head_rows(x4):
    rows = jnp.concatenate(
        [jnp.broadcast_to(x4[s:s + 1, :], (N_HEADS, W_ATTN)) for s in range(x4.shape[0])], axis=0)
    row_h = lax.broadcasted_iota(jnp.int32, rows.shape, 0) % N_HEADS
    lane_h = lax.broadcasted_iota(jnp.int32, rows.shape, 1) // HEAD_DIM
    return jnp.where(row_h == lane_h, rows, 0.0)


def _gather_heads(ref, n_rows):
    return jnp.concatenate([ref[pl.ds(h, n_rows, stride=N_HEADS), :] for h in range(N_HEADS)], axis=-1)


def _dec_qk_kernel(pt_ref, *refs, n_steps, dec_seq):
    pages = refs[:PAGES_PER_STEP]
    q_ref, kn_ref, vn_ref, p_ref, own_ref, kmean, wq = refs[PAGES_PER_STEP:]
    st = pl.program_id(1)
    n_rows = dec_seq * N_HEADS
    n_pages = n_steps * PAGES_PER_STEP
    pages_per_block = MOBA_BLOCK // PAGE_SIZE
    blocks_per_step = PAGES_PER_STEP // pages_per_block
    n_blocks = n_pages // pages_per_block

    @pl.when(st == 0)
    def _():
        wq[...] = _head_rows(q_ref[0])

    w = wq[...]
    for i in range(PAGES_PER_STEP):
        pagesum = jnp.sum(pages[i][...].reshape(PAGE_SIZE, N_HEADS, HEAD_DIM), axis=0)
        blocksum = pagesum if i % pages_per_block == 0 else blocksum + pagesum
        if i % pages_per_block == pages_per_block - 1:
            row0 = pl.multiple_of((st * blocks_per_step + i // pages_per_block) * N_HEADS, N_HEADS)
            kmean[pl.ds(row0, N_HEADS), :] = blocksum * (1.0 / MOBA_BLOCK)
        page = _gather_heads(pages[i], PAGE_SIZE)
        p_ref[0, st * PAGES_PER_STEP + i] = (
            lax.dot_general(w, page, _NT, preferred_element_type=F32) * ATTN_SCALE)

    @pl.when(st == n_steps - 1)
    def _():
        gate = lax.dot_general(w, _gather_heads(kmean, n_blocks), _NT, precision=lax.Precision.HIGHEST,
                               preferred_element_type=F32)
        blk = lax.broadcasted_iota(jnp.int32, gate.shape, 1)
        sel = _top_blocks(gate, blk, n_blocks, 1)
        lo = lax.dot_general(w, kn_ref[0], _NT, preferred_element_type=F32) * ATTN_SCALE
        tok = lax.broadcasted_iota(jnp.int32, lo.shape, 0) // N_HEADS
        key = lax.broadcasted_iota(jnp.int32, lo.shape, 1)
        lo = jnp.where(key <= tok, lo, -jnp.inf)
        m = jnp.max(lo, axis=1, keepdims=True)
        mp = jnp.full((n_rows, PAGE_SIZE), -jnp.inf, F32)
        for pg in range(n_pages):
            keep = sel[:, pg // pages_per_block:pg // pages_per_block + 1] > 0.0
            mp = jnp.maximum(mp, jnp.where(keep, p_ref[0, pg], -jnp.inf))
        m = jnp.maximum(m, jnp.max(mp, axis=1, keepdims=True))
        p_own = jnp.exp(lo - m)
        lsum = jnp.zeros((n_rows, PAGE_SIZE), F32)
        for pg in range(n_pages):
            keep = sel[:, pg // pages_per_block:pg // pages_per_block + 1] > 0.0
            e = jnp.where(keep, jnp.exp(p_ref[0, pg] - m), 0.0)
            p_ref[0, pg] = e
            lsum = lsum + e
        inv = 1.0 / (jnp.sum(lsum, axis=1, keepdims=True) + jnp.sum(p_own, axis=1, keepdims=True))
        for pg in range(n_pages):
            p_ref[0, pg] = p_ref[0, pg] * inv
        p_own = p_own * inv
        vn = vn_ref[0]
        own = p_own[:, 0:1] * vn[0:1, :]
        for jj in range(1, dec_seq):
            own = own + p_own[:, jj:jj + 1] * vn[jj:jj + 1, :]
        own_ref[0] = own


def _page_spec(layer, i, n_pages):
    return pl.BlockSpec(
        (None, None, PAGE_SIZE * N_HEADS, HEAD_DIM),
        lambda b, st, pt: (layer, pt[b * n_pages + st * PAGES_PER_STEP + i], 0, 0))


def _dec_qk(pt_flat, cache_k, layer, q, kn, vn):
    db, dec_seq, _ = q.shape
    n_pages = pt_flat.shape[0] // db
    n_steps = n_pages // PAGES_PER_STEP
    n_rows = dec_seq * N_HEADS
    n_blocks = n_pages * PAGE_SIZE // MOBA_BLOCK
    tok = pl.BlockSpec((1, dec_seq, W_ATTN), lambda b, st, pt: (b, 0, 0))
    grid_spec = pltpu.PrefetchScalarGridSpec(
        num_scalar_prefetch=1,
        grid=(db, n_steps),
        in_specs=[_page_spec(layer, i, n_pages) for i in range(PAGES_PER_STEP)] + [tok, tok, tok],
        out_specs=(pl.BlockSpec((1, n_pages, n_rows, PAGE_SIZE), lambda b, st, pt: (b, 0, 0, 0)),
                   pl.BlockSpec((1, n_rows, W_ATTN), lambda b, st, pt: (b, 0, 0))),
        scratch_shapes=[pltpu.VMEM((n_blocks * N_HEADS, HEAD_DIM), F32),
                        pltpu.VMEM((n_rows, W_ATTN), F32)],
    )
    return pl.pallas_call(
        functools.partial(_dec_qk_kernel, n_steps=n_steps, dec_seq=dec_seq),
        out_shape=(jax.ShapeDtypeStruct((db, n_pages, n_rows, PAGE_SIZE), F32),
                   jax.ShapeDtypeStruct((db, n_rows, W_ATTN), F32)),
        grid_spec=grid_spec,
        compiler_params=pltpu.CompilerParams(
            dimension_semantics=("parallel", "arbitrary"), vmem_limit_bytes=VMEM_LIMIT),
        name="dec_qk",
    )(pt_flat, *([cache_k] * PAGES_PER_STEP), q, kn, vn)


def _dec_pv_kernel(pt_ref, *refs, n_steps, dec_seq):
    pages = refs[:PAGES_PER_STEP]
    p_ref, own_ref, g_ref, o_ref, acc = refs[PAGES_PER_STEP:]
    st = pl.program_id(1)

    @pl.when(st == 0)
    def _():
        acc[...] = own_ref[0]

    a = acc[...]
    for i in range(PAGES_PER_STEP):
        a = a + jnp.dot(p_ref[0, st * PAGES_PER_STEP + i], _gather_heads(pages[i], PAGE_SIZE),
                        preferred_element_type=F32)
    acc[...] = a

    @pl.when(st == n_steps - 1)
    def _():
        row_h = lax.broadcasted_iota(jnp.int32, a.shape, 0) % N_HEADS
        lane_h = lax.broadcasted_iota(jnp.int32, a.shape, 1) // HEAD_DIM
        diag = jnp.where(row_h == lane_h, a, 0.0)
        heads = jnp.concatenate(
            [jnp.sum(diag[s * N_HEADS:(s + 1) * N_HEADS], axis=0, keepdims=True) for s in range(dec_seq)], axis=0)
        o_ref[0] = heads * _silu(g_ref[0])


def _dec_pv(pt_flat, cache_v, layer, probs, own, g):
    db, n_pages, n_rows, _ = probs.shape
    dec_seq = n_rows // N_HEADS
    n_steps = n_pages // PAGES_PER_STEP
    tok = pl.BlockSpec((1, dec_seq, W_ATTN), lambda b, st, pt: (b, 0, 0))
    grid_spec = pltpu.PrefetchScalarGridSpec(
        num_scalar_prefetch=1,
        grid=(db, n_steps),
        in_specs=[_page_spec(layer, i, n_pages) for i in range(PAGES_PER_STEP)] + [
            pl.BlockSpec((1, n_pages, n_rows, PAGE_SIZE), lambda b, st, pt: (b, 0, 0, 0)),
            pl.BlockSpec((1, n_rows, W_ATTN), lambda b, st, pt: (b, 0, 0)),
            tok],
        out_specs=tok,
        scratch_shapes=[pltpu.VMEM((n_rows, W_ATTN), F32)],
    )
    return pl.pallas_call(
        functools.partial(_dec_pv_kernel, n_steps=n_steps, dec_seq=dec_seq),
        out_shape=jax.ShapeDtypeStruct((db, dec_seq, W_ATTN), F32),
        grid_spec=grid_spec,
        compiler_params=pltpu.CompilerParams(
            dimension_semantics=("parallel", "arbitrary"), vmem_limit_bytes=VMEM_LIMIT),
        name="dec_pv",
    )(pt_flat, *([cache_v] * PAGES_PER_STEP), probs, own, g)


def _dec_attention(pt_flat, cache_k, cache_v, layer, q, kn, vn, g):
    depth, n_phys = cache_k.shape[:2]
    view = (depth, n_phys, PAGE_SIZE * N_HEADS, HEAD_DIM)
    probs, own = _dec_qk(pt_flat, cache_k.reshape(view), layer, q, kn, vn)
    return _dec_pv(pt_flat, cache_v.reshape(view), layer, probs, own, g)


def _outproj_kernel(ab_ref, c_ref, x_ref, wa_ref, wc_ref, y_ref, mix_scr):
    j = pl.program_id(1)

    @pl.when(j == 0)
    def _():
        mix_scr[:, 0:W_CONV + W_POOL] = ab_ref[...].astype(BF16)
        mix_scr[:, W_CONV + W_POOL:] = c_ref[...].astype(BF16)

    y = jnp.dot(mix_scr[:, 0:W_CONV + W_POOL], wa_ref[...], preferred_element_type=F32)
    y = y + jnp.dot(mix_scr[:, W_CONV + W_POOL:], wc_ref[...], preferred_element_type=F32)
    y_ref[...] = x_ref[...] + y


def _outproj(mix_ab, mix_c, x, w_out_bf, tm):
    m = x.shape[0]
    half = W_CONV + W_POOL
    return pl.pallas_call(
        _outproj_kernel,
        out_shape=jax.ShapeDtypeStruct((m, D_MODEL), F32),
        grid=(m // tm, D_MODEL // COL),
        in_specs=[pl.BlockSpec((tm, half), lambda i, j: (i, 0)),
                  pl.BlockSpec((tm, W_ATTN), lambda i, j: (i, 0)),
                  pl.BlockSpec((tm, COL), lambda i, j: (i, j)),
                  pl.BlockSpec((half, COL), lambda i, j: (0, j)),
                  pl.BlockSpec((W_ATTN, COL), lambda i, j: (1, j))],
        out_specs=pl.BlockSpec((tm, COL), lambda i, j: (i, j)),
        scratch_shapes=[pltpu.VMEM((tm, half + W_ATTN), BF16)],
        compiler_params=pltpu.CompilerParams(
            dimension_semantics=("parallel", "arbitrary"), vmem_limit_bytes=VMEM_LIMIT),
        name="outproj",
    )(mix_ab, mix_c, x, w_out_bf, w_out_bf)


def _rope_tables(pos):
    half = HEAD_DIM // 2
    inv = ROPE_THETA ** (-jnp.arange(half, dtype=F32) / half)
    ang = pos.astype(F32)[:, None] * inv[None, :]
    cos, sin = jnp.cos(ang), jnp.sin(ang)
    return jnp.concatenate([cos, cos], axis=-1), jnp.concatenate([-sin, sin], axis=-1)


def kernel(x_prompt, x_sample, cache_k, cache_v, state_conv, state_pool, page_table, norm_g, w_in, w_out,
           conv_w, conv_b, conv_ln_g, conv_ln_b, pool_w, pool_scale, q_norm_g, k_norm_g):
    batch, seq, _ = x_prompt.shape
    db, dec_seq, _ = x_sample.shape
    depth = w_in.shape[0]
    past_len = page_table.shape[1] * PAGE_SIZE

    hp = x_prompt.reshape(batch * seq, D_MODEL)
    hs = x_sample.reshape(db * dec_seq, D_MODEL)
    cos_p, sin_p = _rope_tables(jnp.arange(seq, dtype=jnp.int32))
    cos_s, sin_s = _rope_tables(past_len + jnp.arange(db * dec_seq, dtype=jnp.int32) % dec_seq)
    pt_flat = page_table.reshape(-1)
    zero_cctx = jnp.zeros((batch, CONV_CTX, W_CONV), F32)
    zero_pctx = jnp.zeros((batch, POOL_CTX, W_POOL), F32)

    states = [[] for _ in range(4)]
    kv_p = kv_s = None
    for l in range(depth):
        w_in_bf = w_in[l].astype(BF16)
        w_out_bf = w_out[l].astype(BF16)
        pool_w_bf = pool_w[l].astype(BF16)
        small = (conv_w[l], conv_b[l][None], conv_ln_g[l][None], conv_ln_b[l][None], pool_w_bf, pool_scale[l][None])
        ng, qg, kg = norm_g[l][None], q_norm_g[l][None], k_norm_g[l][None]

        misc, q, k, v, g, *kv_p = _proj(hp, ng, w_in_bf, cos_p, sin_p, qg, kg, tm=512,
                                        layer=l, depth=depth, kv_fin=kv_p)
        mix_ab, c_new, p_new = _convpool(misc.reshape(batch, seq, W_MISC), zero_cctx, zero_pctx, *small,
                                         tt=256, rows=64, pos0=0)
        mix_c = _moba(q, k, v, g, batch, seq)
        hp = _outproj(mix_ab.reshape(batch * seq, -1), mix_c, hp, w_out_bf, tm=1024)
        states[0].append(c_new)
        states[1].append(p_new)

        misc, q, k, v, g, *kv_s = _proj(hs, ng, w_in_bf, cos_s, sin_s, qg, kg, tm=db * dec_seq,
                                        layer=l, depth=depth, kv_fin=kv_s)
        mix_ab, c_new, p_new = _convpool(misc.reshape(db, dec_seq, W_MISC), state_conv[l], state_pool[l], *small,
                                         tt=dec_seq, rows=dec_seq, pos0=past_len)
        q3, k3, v3, g3 = (a.reshape(db, dec_seq, W_ATTN) for a in (q, k, v, g))
        mix_c = _dec_attention(pt_flat, cache_k, cache_v, l, q3, k3, v3, g3)
        hs = _outproj(mix_ab.reshape(db * dec_seq, -1), mix_c.reshape(db * dec_seq, W_ATTN), hs, w_out_bf,
                      tm=db * dec_seq)
        states[2].append(c_new)
        states[3].append(p_new)

    kv_shape_p = (depth, batch, seq, N_HEADS, HEAD_DIM)
    kv_shape_s = (depth, db, dec_seq, N_HEADS, HEAD_DIM)
    return (hp.reshape(batch, seq, D_MODEL), hs.reshape(db, dec_seq, D_MODEL),
            kv_p[0].reshape(kv_shape_p), kv_p[1].reshape(kv_shape_p),
            jnp.stack(states[0]), jnp.stack(states[1]),
            kv_s[0].reshape(kv_shape_s), kv_s[1].reshape(kv_shape_s),
            jnp.stack(states[2]), jnp.stack(states[3]))
```

```python
import functools
import math

import jax
import jax.numpy as jnp
from jax import lax
from jax.experimental import pallas as pl
from jax.experimental.pallas import tpu as pltpu

F32 = jnp.float32
BF16 = jnp.bfloat16

D_MODEL = 2048
N_HEADS = 8
HEAD_DIM = 128
W_ATTN = N_HEADS * HEAD_DIM
W_CONV = 512
W_POOL = 512
CONV_WIDTH = 31
CONV_CTX = CONV_WIDTH - 1
POOL_WINDOWS = (2, 4, 8, 16)
POOL_GROUP = W_POOL // len(POOL_WINDOWS)
POOL_CTX = max(POOL_WINDOWS) - 1
MOBA_BLOCK = 256
MOBA_TOPK = 3
PAGE_SIZE = 128
ROPE_THETA = 10000.0
EPS = 1e-6
W_MISC = 3 * W_CONV + 2 * W_POOL
D_IN = W_MISC + 4 * W_ATTN
ATTN_SCALE = HEAD_DIM ** -0.5
LOG2E = math.log2(math.e)

COL = 512
COL_HEADS = COL // HEAD_DIM
N_COL = D_IN // COL
J_Q, J_K, J_V, J_G = 5, 7, 9, 11
CONV_PAD = 32
POOL_PAD = 16
PROJ_CHUNK = 256
MOBA_HEADS = 4
PAGES_PER_STEP = 16
VMEM_LIMIT = 56 * 1024 * 1024

_NT = (((1,), (1,)), ((), ()))


def _silu(x):
    return x * jax.nn.sigmoid(x)


def _norm_rope(z, gain, cos, sin):
    outs = []
    for hh in range(COL_HEADS):
        xh = z[:, hh * HEAD_DIM:(hh + 1) * HEAD_DIM]
        ms = jnp.mean(xh * xh, axis=-1, keepdims=True)
        y = xh * lax.rsqrt(ms + EPS) * gain
        outs.append(y * cos + pltpu.roll(y, HEAD_DIM // 2, 1) * sin)
    return jnp.concatenate(outs, axis=-1)


def _store_heads(fin_ref, tile, first_row):
    rows = tile.shape[0]
    for hh in range(COL_HEADS):
        fin_ref[pl.ds(first_row + hh, rows, stride=N_HEADS), :] = tile[:, hh * HEAD_DIM:(hh + 1) * HEAD_DIM]


def _proj_kernel(*refs, aliased, chunk):
    x_ref, ng_ref, w_ref, cos_ref, sin_ref, qg_ref, kg_ref = refs[:7]
    z_ref, kfin_ref, vfin_ref, h_scr = refs[7 + 2 * aliased:]
    j = pl.program_id(1)
    tm = x_ref.shape[0]

    @pl.when(j == 0)
    def _():
        x = x_ref[...]
        ms = jnp.mean(x * x, axis=-1, keepdims=True)
        h_scr[...] = (x * lax.rsqrt(ms + EPS) * ng_ref[...]).astype(BF16)

    def columns(epilogue):
        for r0 in range(0, tm, chunk):
            rows = slice(r0, r0 + chunk)
            z_ref[rows, :] = epilogue(jnp.dot(h_scr[rows, :], w_ref[...], preferred_element_type=F32), r0)

    def rope_with(gain_ref):
        def epilogue(z, r0):
            return _norm_rope(z, gain_ref[...], cos_ref[r0:r0 + chunk, :], sin_ref[r0:r0 + chunk, :])
        return epilogue

    def copy_to(fin_ref, j0, inner=lambda z, r0: z):
        def epilogue(z, r0):
            z = inner(z, r0)
            _store_heads(fin_ref, z, r0 * N_HEADS + (j - j0) * COL_HEADS)
            return z
        return epilogue

    @pl.when((j < J_Q) | (j >= J_G))
    def _():
        columns(lambda z, r0: z)

    @pl.when((j >= J_Q) & (j < J_K))
    def _():
        columns(rope_with(qg_ref))

    @pl.when((j >= J_K) & (j < J_V))
    def _():
        columns(copy_to(kfin_ref, J_K, rope_with(kg_ref)))

    @pl.when((j >= J_V) & (j < J_G))
    def _():
        columns(copy_to(vfin_ref, J_V))


def _proj(x, norm_g, w_in_bf, cos, sin, qg, kg, tm, layer=0, depth=1, kv_fin=None):
    m = x.shape[0]
    n_pos_tiles = cos.shape[0] // tm
    aliased = kv_fin is not None
    fin_shape = jax.ShapeDtypeStruct((depth, m * N_HEADS, HEAD_DIM), F32)
    fin_spec = pl.BlockSpec((None, tm * N_HEADS, HEAD_DIM), lambda i, j: (layer, i, 0))
    in_specs = [
        pl.BlockSpec((tm, D_MODEL), lambda i, j: (i, 0)),
        pl.BlockSpec((1, D_MODEL), lambda i, j: (0, 0)),
        pl.BlockSpec((None, D_MODEL, COL), lambda i, j: (layer, 0, j)),
        pl.BlockSpec((tm, HEAD_DIM), lambda i, j: (i % n_pos_tiles, 0)),
        pl.BlockSpec((tm, HEAD_DIM), lambda i, j: (i % n_pos_tiles, 0)),
        pl.BlockSpec((1, HEAD_DIM), lambda i, j: (0, 0)),
        pl.BlockSpec((1, HEAD_DIM), lambda i, j: (0, 0)),
    ]
    args = [x, norm_g, w_in_bf, cos, sin, qg, kg]
    aliases = {}
    if aliased:
        in_specs += [pl.BlockSpec(memory_space=pl.ANY), pl.BlockSpec(memory_space=pl.ANY)]
        args += list(kv_fin)
        aliases = {7: 1, 8: 2}
    return pl.pallas_call(
        functools.partial(_proj_kernel, aliased=aliased, chunk=min(tm, PROJ_CHUNK)),
        out_shape=(jax.ShapeDtypeStruct((m, D_IN), F32), fin_shape, fin_shape),
        grid=(m // tm, N_COL),
        in_specs=in_specs,
        out_specs=(pl.BlockSpec((tm, COL), lambda i, j: (i, j)), fin_spec, fin_spec),
        scratch_shapes=[pltpu.VMEM((tm, D_MODEL), BF16)],
        input_output_aliases=aliases,
        compiler_params=pltpu.CompilerParams(
            dimension_semantics=("parallel", "arbitrary"), vmem_limit_bytes=VMEM_LIMIT),
        name="proj",
    )(*args)


def _causal_conv(cs, cw_ref, cb_ref, base, rows, lane_group):
    lanes = slice(lane_group * HEAD_DIM, (lane_group + 1) * HEAD_DIM)
    acc = jnp.broadcast_to(cb_ref[:, lanes], (rows, HEAD_DIM))
    first = CONV_PAD - CONV_CTX
    for r in range(8):
        part = None
        for d in range(r, CONV_PAD + 1, 8):
            if d < first:
                continue
            term = cw_ref[d - first:d - first + 1, lanes] * cs[base + d - r:base + d - r + rows + 8, lanes]
            part = term if part is None else part + term
        acc = acc + part[r:r + rows]
    return acc


def _convpool_kernel(a_ref, b_ref, ga_ref, u_ref, gb_ref, cctx_ref, pctx_ref,
                     cw_ref, cb_ref, lg_ref, lb_ref, pw_ref, ps_ref,
                     mix_ref, cnew_ref, pnew_ref, cs, ps, *, tt, rows, n_t, pos0):
    t = pl.program_id(1)

    @pl.when(t == 0)
    def _():
        cs[0:CONV_PAD - CONV_CTX, :] = jnp.zeros((CONV_PAD - CONV_CTX, W_CONV), F32)
        cs[CONV_PAD - CONV_CTX:CONV_PAD, :] = cctx_ref[0]
        cs[CONV_PAD + tt:CONV_PAD + tt + 8, :] = jnp.zeros((8, W_CONV), F32)
        ps[0:POOL_PAD - POOL_CTX, :] = jnp.zeros((POOL_PAD - POOL_CTX, W_POOL), F32)
        ps[POOL_PAD - POOL_CTX:POOL_PAD, :] = pctx_ref[0]

    cs[CONV_PAD:CONV_PAD + tt, :] = a_ref[0] * jax.nn.sigmoid(b_ref[0])
    ps[POOL_PAD:POOL_PAD + tt, :] = u_ref[0]

    for base in range(0, tt, rows):
        acc = jnp.concatenate([_causal_conv(cs, cw_ref, cb_ref, base, rows, lg)
                               for lg in range(W_CONV // HEAD_DIM)], axis=-1)
        mu =jnp.mean(acc, axis=-1, keepdims=True)
        d = acc - mu
        var = jnp.mean(d * d, axis=-1, keepdims=True)
        ya = _silu(d * lax.rsqrt(var + EPS) * lg_ref[...] + lb_ref[...])
        mix_ref[0, pl.ds(base, rows), 0:W_CONV] = ya * _silu(ga_ref[0, pl.ds(base, rows), :])
        pos = pos0 + t * tt + base + lax.broadcasted_iota(jnp.int32, (rows, 1), 0)
        for g, w in enumerate(POOL_WINDOWS):
            sl = slice(g * POOL_GROUP, (g + 1) * POOL_GROUP)
            tok = ps[pl.ds(base + POOL_PAD, rows), sl]
            s = tok
            for i in range(1, w):
                s = s + ps[pl.ds(base + POOL_PAD - i, rows), sl]
            cnt = jnp.minimum(w, pos + 1).astype(F32)
            p = s / cnt - tok
            yb = jnp.dot(p.astype(BF16), pw_ref[g], preferred_element_type=F32) * ps_ref[:, sl]
            mix_ref[0, pl.ds(base, rows), W_CONV + g * POOL_GROUP:W_CONV + (g + 1) * POOL_GROUP] = (
                yb * _silu(gb_ref[0, pl.ds(base, rows), sl]))

    @pl.when(t == n_t - 1)
    def _():
        cnew_ref[0] = cs[tt + CONV_PAD - CONV_CTX:tt + CONV_PAD, :]
        pnew_ref[0] = ps[tt + POOL_PAD - POOL_CTX:tt + POOL_PAD, :]

    if n_t > 1:
        @pl.when(t < n_t - 1)
        def _():
            cs[0:CONV_PAD, :] = cs[tt:tt + CONV_PAD, :]
            ps[0:POOL_PAD, :] = ps[tt:tt + POOL_PAD, :]


def _convpool(misc, cctx, pctx, conv_w, conv_b, ln_g, ln_b, pool_w_bf, pool_scale, tt, rows, pos0):
    n, t_len, _ = misc.shape
    n_t = t_len // tt

    def col(c):
        return pl.BlockSpec((1, tt, COL), lambda b, t: (b, t, c))

    def full(shape):
        return pl.BlockSpec(shape, lambda b, t: (0,) * len(shape))

    return pl.pallas_call(
        functools.partial(_convpool_kernel, tt=tt, rows=rows, n_t=n_t, pos0=pos0),
        out_shape=(jax.ShapeDtypeStruct((n, t_len, W_CONV + W_POOL), F32),
                   jax.ShapeDtypeStruct((n, CONV_CTX, W_CONV), F32),
                   jax.ShapeDtypeStruct((n, POOL_CTX, W_POOL), F32)),
        grid=(n, n_t),
        in_specs=[col(0), col(1), col(2), col(3), col(4),
                  pl.BlockSpec((1, CONV_CTX, W_CONV), lambda b, t: (b, 0, 0)),
                  pl.BlockSpec((1, POOL_CTX, W_POOL), lambda b, t: (b, 0, 0)),
                  full((CONV_WIDTH, W_CONV)), full((1, W_CONV)), full((1, W_CONV)), full((1, W_CONV)),
                  full((len(POOL_WINDOWS), POOL_GROUP, POOL_GROUP)), full((1, W_POOL))],
        out_specs=(pl.BlockSpec((1, tt, W_CONV + W_POOL), lambda b, t: (b, t, 0)),
                   pl.BlockSpec((1, CONV_CTX, W_CONV), lambda b, t: (b, 0, 0)),
                   pl.BlockSpec((1, POOL_CTX, W_POOL), lambda b, t: (b, 0, 0))),
        scratch_shapes=[pltpu.VMEM((CONV_PAD + tt + 8, W_CONV), F32),
                        pltpu.VMEM((POOL_PAD + tt, W_POOL), F32)],
        compiler_params=pltpu.CompilerParams(
            dimension_semantics=("parallel", "arbitrary"), vmem_limit_bytes=VMEM_LIMIT),
        name="convpool",
    )(misc, misc, misc, misc, misc, cctx, pctx, conv_w, conv_b, ln_g, ln_b, pool_w_bf, pool_scale)


def _top_blocks(gate, idx, n_valid, axis):
    n_blocks = gate.shape[axis]
    gate = jnp.where(idx < n_valid, gate, -jnp.inf)
    sel = jnp.zeros(gate.shape, F32)
    for _ in range(MOBA_TOPK):
        best = jnp.max(gate, axis=axis, keepdims=True)
        first = jnp.min(jnp.where(gate == best, idx, n_blocks), axis=axis, keepdims=True)
        pick = idx == first
        sel = jnp.where(pick, 1.0, sel)
        gate = jnp.where(pick, -jnp.inf, gate)
    return jnp.where(idx < n_valid, sel, 0.0)


def _moba_kernel(q_ref, k_ref, v_ref, g_ref, o_ref, kbf, vtbf, kmean, sel_scr, acc_scr, qs_scr, s_even, s_odd,
                 *, n_blocks):
    c = pl.program_id(2)

    @pl.when(c == 0)
    def _():
        def prep(n, carry):
            rows = pl.ds(pl.multiple_of(n * MOBA_BLOCK, MOBA_BLOCK), MOBA_BLOCK)
            for hh in range(MOBA_HEADS):
                lanes = slice(hh * HEAD_DIM, (hh + 1) * HEAD_DIM)
                kb = k_ref[rows, lanes]
                kbf[hh, n] = kb.astype(BF16)
                kmean[hh, pl.ds(n, 1), :] = jnp.sum(kb, axis=0, keepdims=True) * (1.0 / MOBA_BLOCK)
                vtbf[hh, n] = v_ref[rows, lanes].T.astype(BF16)
            return carry
        lax.fori_loop(0, n_blocks, prep, 0)

    qf = [q_ref[:, hh * HEAD_DIM:(hh + 1) * HEAD_DIM] for hh in range(MOBA_HEADS)]
    for hh in range(MOBA_HEADS):
        qs_scr[hh] = (qf[hh] * (ATTN_SCALE * LOG2E)).astype(BF16)
    scores = [lax.dot_general(kbf[hh, c], qs_scr[hh], _NT, preferred_element_type=F32)
              for hh in range(MOBA_HEADS)]
    gates = [lax.dot_general(kmean[hh], qf[hh], _NT, precision=lax.Precision.HIGHEST,
                             preferred_element_type=F32) for hh in range(MOBA_HEADS)]
    kpos = lax.broadcasted_iota(jnp.int32, scores[0].shape, 0)
    qpos = lax.broadcasted_iota(jnp.int32, scores[0].shape, 1)
    blk = lax.broadcasted_iota(jnp.int32, gates[0].shape, 0)
    stats = []
    for hh in range(MOBA_HEADS):
        s = jnp.where(kpos <= qpos, scores[hh], -jnp.inf)
        m0 = jnp.max(s, axis=0, keepdims=True)
        p = jnp.exp2(s - m0)
        stats.append((m0, jnp.sum(p, axis=0, keepdims=True)))
        acc_scr[hh] = jnp.dot(vtbf[hh, c], p.astype(BF16), preferred_element_type=F32)
    for hh in range(MOBA_HEADS):
        sel_scr[hh] = _top_blocks(gates[hh], blk, c, 0)

    def score(buf, n):
        for hh in range(MOBA_HEADS):
            buf[hh] = lax.dot_general(kbf[hh, n], qs_scr[hh], _NT, preferred_element_type=F32)

    def reduce(buf, n, carry):
        new = []
        for hh in range(MOBA_HEADS):
            m, l = carry[hh]
            s = jnp.where(sel_scr[hh, pl.ds(n, 1), :] > 0.0, buf[hh], -jnp.inf)
            m_new = jnp.maximum(m, jnp.max(s, axis=0, keepdims=True))
            alpha = jnp.exp2(m - m_new)
            p = jnp.exp2(s - m_new)
            acc_scr[hh] = alpha * acc_scr[hh] + jnp.dot(vtbf[hh, n], p.astype(BF16), preferred_element_type=F32)
            new.append((m_new, alpha * l + jnp.sum(p, axis=0, keepdims=True)))
        return tuple(new)

    def past(i, carry):
        n = 2 * i
        score(s_odd, n + 1)
        carry = reduce(s_even, n, carry)
        score(s_even, jnp.minimum(n + 2, n_blocks - 1))
        return reduce(s_odd, n + 1, carry)

    score(s_even, 0)
    stats = lax.fori_loop(0, (c + 1) // 2, past, tuple(stats))
    for hh in range(MOBA_HEADS):
        lanes = slice(hh * HEAD_DIM, (hh + 1) * HEAD_DIM)
        o_ref[:, lanes] = (acc_scr[hh] / stats[hh][1]).T * _silu(g_ref[:, lanes])


def _moba(z, batch, t_len):
    assert MOBA_HEADS == COL_HEADS
    m = z.shape[0]
    n_blocks = t_len // MOBA_BLOCK

    def rows(j0):
        return pl.BlockSpec((MOBA_BLOCK, COL), lambda b, h, c: (b * n_blocks + c, j0 + h))

    def seq(j0):
        return pl.BlockSpec((t_len, COL), lambda b, h, c: (b, j0 + h))

    return pl.pallas_call(
        functools.partial(_moba_kernel, n_blocks=n_blocks),
        out_shape=jax.ShapeDtypeStruct((m, W_ATTN), F32),
        grid=(batch, N_HEADS // MOBA_HEADS, n_blocks),
        in_specs=[rows(J_Q), seq(J_K), seq(J_V), rows(J_G)],
        out_specs=rows(0),
        scratch_shapes=[pltpu.VMEM((MOBA_HEADS, n_blocks, MOBA_BLOCK, HEAD_DIM), BF16),
                        pltpu.VMEM((MOBA_HEADS, n_blocks, HEAD_DIM, MOBA_BLOCK), BF16),
                        pltpu.VMEM((MOBA_HEADS, n_blocks, HEAD_DIM), F32),
                        pltpu.VMEM((MOBA_HEADS, n_blocks, MOBA_BLOCK), F32),
                        pltpu.VMEM((MOBA_HEADS, HEAD_DIM, MOBA_BLOCK), F32),
                        pltpu.VMEM((MOBA_HEADS, MOBA_BLOCK, HEAD_DIM), BF16),
                        pltpu.VMEM((MOBA_HEADS, MOBA_BLOCK, MOBA_BLOCK), F32),
                        pltpu.VMEM((MOBA_HEADS, MOBA_BLOCK, MOBA_BLOCK), F32)],
        compiler_params=pltpu.CompilerParams(
            dimension_semantics=("parallel", "parallel", "arbitrary"), vmem_limit_bytes=VMEM_LIMIT),
        name="moba",
    )(z, z, z, z)


def _head_rows(x4):
    rows = jnp.concatenate(
        [jnp.broadcast_to(x4[s:s + 1, :], (N_HEADS, W_ATTN)) for s in range(x4.shape[0])], axis=0)
    row_h = lax.broadcasted_iota(jnp.int32, rows.shape, 0) % N_HEADS
    lane_h = lax.broadcasted_iota(jnp.int32, rows.shape, 1) // HEAD_DIM
    return jnp.where(row_h == lane_h, rows, 0.0)


def _gather_heads(ref, n_rows):
    return jnp.concatenate([ref[pl.ds(h, n_rows, stride=N_HEADS), :] for h in range(N_HEADS)], axis=-1)


def _dec_qk_kernel(pt_ref, *refs, n_steps, dec_seq):
    pages = refs[:PAGES_PER_STEP]
    q_ref, kn_ref, vn_ref, p_ref, own_ref, kmean, wq = refs[PAGES_PER_STEP:]
    st = pl.program_id(1)
    n_rows = dec_seq * N_HEADS
    n_pages = n_steps * PAGES_PER_STEP
    pages_per_block = MOBA_BLOCK // PAGE_SIZE
    blocks_per_step = PAGES_PER_STEP // pages_per_block
    n_blocks = n_pages // pages_per_block

    @pl.when(st == 0)
    def _():
        wq[...] = _head_rows(q_ref[0])

    w = wq[...]
    for i in range(PAGES_PER_STEP):
        page = _gather_heads(pages[i], PAGE_SIZE)
        pagesum = jnp.sum(page.reshape(PAGE_SIZE // 8, 8, W_ATTN), axis=0)
        blocksum = pagesum if i % pages_per_block == 0 else blocksum + pagesum
        if i % pages_per_block == pages_per_block - 1:
            kmean[pl.ds(st * blocks_per_step + i // pages_per_block, 1), :] = (
                jnp.sum(blocksum, axis=0, keepdims=True) * (1.0 / MOBA_BLOCK))
        p_ref[0, st * PAGES_PER_STEP + i] = (
            lax.dot_general(w, page, _NT, preferred_element_type=F32) * ATTN_SCALE)

    @pl.when(st == n_steps - 1)
    def _():
        gate = lax.dot_general(w, kmean[...], _NT, precision=lax.Precision.HIGHEST,
                               preferred_element_type=F32)
        blk = lax.broadcasted_iota(jnp.int32, gate.shape, 1)
        sel = _top_blocks(gate, blk, n_blocks, 1)
        lo = lax.dot_general(w, kn_ref[0], _NT, preferred_element_type=F32) * ATTN_SCALE
        tok = lax.broadcasted_iota(jnp.int32, lo.shape, 0) // N_HEADS
        key = lax.broadcasted_iota(jnp.int32, lo.shape, 1)
        lo = jnp.where(key <= tok, lo, -jnp.inf)
        m = jnp.max(lo, axis=1, keepdims=True)
        mp = jnp.full((n_rows, PAGE_SIZE), -jnp.inf, F32)
        for pg in range(n_pages):
            keep = sel[:, pg // pages_per_block:pg // pages_per_block + 1] > 0.0
            mp = jnp.maximum(mp, jnp.where(keep, p_ref[0, pg], -jnp.inf))
        m = jnp.maximum(m, jnp.max(mp, axis=1, keepdims=True))
        p_own = jnp.exp(lo - m)
        lsum = jnp.zeros((n_rows, PAGE_SIZE), F32)
        for pg in range(n_pages):
            keep = sel[:, pg // pages_per_block:pg // pages_per_block + 1] > 0.0
            e = jnp.where(keep, jnp.exp(p_ref[0, pg] - m), 0.0)
            p_ref[0, pg] = e
            lsum = lsum + e
        inv = 1.0 / (jnp.sum(lsum, axis=1, keepdims=True) + jnp.sum(p_own, axis=1, keepdims=True))
        for pg in range(n_pages):
            p_ref[0, pg] = p_ref[0, pg] * inv
        p_own = p_own * inv
        vn = vn_ref[0]
        own = p_own[:, 0:1] * vn[0:1, :]
        for jj in range(1, dec_seq):
            own = own + p_own[:, jj:jj + 1] * vn[jj:jj + 1, :]
        own_ref[0] = own


def _page_spec(layer, i, n_pages):
    return pl.BlockSpec(
        (None, None, PAGE_SIZE * N_HEADS, HEAD_DIM),
        lambda b, st, pt: (layer, pt[b * n_pages + st * PAGES_PER_STEP + i], 0, 0))


def _dec_qk(pt_flat, cache_k, layer, q, kn, vn):
    db, dec_seq, _ = q.shape
    n_pages = pt_flat.shape[0] // db
    n_steps = n_pages // PAGES_PER_STEP
    n_rows = dec_seq * N_HEADS
    n_blocks = n_pages * PAGE_SIZE // MOBA_BLOCK
    tok = pl.BlockSpec((1, dec_seq, W_ATTN), lambda b, st, pt: (b, 0, 0))
    grid_spec = pltpu.PrefetchScalarGridSpec(
        num_scalar_prefetch=1,
        grid=(db, n_steps),
        in_specs=[_page_spec(layer, i, n_pages) for i in range(PAGES_PER_STEP)] + [tok, tok, tok],
        out_specs=(pl.BlockSpec((1, n_pages, n_rows, PAGE_SIZE), lambda b, st, pt: (b, 0, 0, 0)),
                   pl.BlockSpec((1, n_rows, W_ATTN), lambda b, st, pt: (b, 0, 0))),
        scratch_shapes=[pltpu.VMEM((n_blocks, W_ATTN), F32),
                        pltpu.VMEM((n_rows, W_ATTN), F32)],
    )
    return pl.pallas_call(
        functools.partial(_dec_qk_kernel, n_steps=n_steps, dec_seq=dec_seq),
        out_shape=(jax.ShapeDtypeStruct((db, n_pages, n_rows, PAGE_SIZE), F32),
                   jax.ShapeDtypeStruct((db, n_rows, W_ATTN), F32)),
        grid_spec=grid_spec,
        compiler_params=pltpu.CompilerParams(
            dimension_semantics=("parallel", "arbitrary"), vmem_limit_bytes=VMEM_LIMIT),
        name="dec_qk",
    )(pt_flat, *([cache_k] * PAGES_PER_STEP), q, kn, vn)


def _dec_pv_kernel(pt_ref, *refs, n_steps, dec_seq):
    pages = refs[:PAGES_PER_STEP]
    p_ref, own_ref, g_ref, o_ref, acc = refs[PAGES_PER_STEP:]
    st = pl.program_id(1)

    @pl.when(st == 0)
    def _():
        acc[...] = own_ref[0]

    a = acc[...]
    for i in range(PAGES_PER_STEP):
        a = a + jnp.dot(p_ref[0, st * PAGES_PER_STEP + i], _gather_heads(pages[i], PAGE_SIZE),
                        preferred_element_type=F32)
    acc[...] = a

    @pl.when(st == n_steps - 1)
    def _():
        row_h = lax.broadcasted_iota(jnp.int32, a.shape, 0) % N_HEADS
        lane_h = lax.broadcasted_iota(jnp.int32, a.shape, 1) // HEAD_DIM
        diag = jnp.where(row_h == lane_h, a, 0.0)
        heads = jnp.concatenate(
            [jnp.sum(diag[s * N_HEADS:(s + 1) * N_HEADS], axis=0, keepdims=True) for s in range(dec_seq)], axis=0)
        o_ref[0] = heads * _silu(g_ref[0])


def _dec_pv(pt_flat, cache_v, layer, probs, own, g):
    db, n_pages, n_rows, _ = probs.shape
    dec_seq = n_rows // N_HEADS
    n_steps = n_pages // PAGES_PER_STEP
    tok = pl.BlockSpec((1, dec_seq, W_ATTN), lambda b, st, pt: (b, 0, 0))
    grid_spec = pltpu.PrefetchScalarGridSpec(
        num_scalar_prefetch=1,
        grid=(db, n_steps),
        in_specs=[_page_spec(layer, i, n_pages) for i in range(PAGES_PER_STEP)] + [
            pl.BlockSpec((1, n_pages, n_rows, PAGE_SIZE), lambda b, st, pt: (b, 0, 0, 0)),
            pl.BlockSpec((1, n_rows, W_ATTN), lambda b, st, pt: (b, 0, 0)),
            tok],
        out_specs=tok,
        scratch_shapes=[pltpu.VMEM((n_rows, W_ATTN), F32)],
    )
    return pl.pallas_call(
        functools.partial(_dec_pv_kernel, n_steps=n_steps, dec_seq=dec_seq),
        out_shape=jax.ShapeDtypeStruct((db, dec_seq, W_ATTN), F32),
        grid_spec=grid_spec,
        compiler_params=pltpu.CompilerParams(
            dimension_semantics=("parallel", "arbitrary"), vmem_limit_bytes=VMEM_LIMIT),
        name="dec_pv",
    )(pt_flat, *([cache_v] * PAGES_PER_STEP), probs, own, g)


def _dec_attention(pt_flat, cache_k, cache_v, layer, q, kn, vn, g):
    depth, n_phys = cache_k.shape[:2]
    view = (depth, n_phys, PAGE_SIZE * N_HEADS, HEAD_DIM)
    probs, own = _dec_qk(pt_flat, cache_k.reshape(view), layer, q, kn, vn)
    return _dec_pv(pt_flat, cache_v.reshape(view), layer, probs, own, g)


def _outproj_kernel(ab_ref, c_ref, x_ref, wa_ref, wc_ref, y_ref, mix_scr):
    j = pl.program_id(1)

    @pl.when(j == 0)
    def _():
        mix_scr[:, 0:W_CONV + W_POOL] = ab_ref[...].astype(BF16)
        mix_scr[:, W_CONV + W_POOL:] = c_ref[...].astype(BF16)

    y = jnp.dot(mix_scr[:, 0:W_CONV + W_POOL], wa_ref[...], preferred_element_type=F32)
    y = y + jnp.dot(mix_scr[:, W_CONV + W_POOL:], wc_ref[...], preferred_element_type=F32)
    y_ref[...] = x_ref[...] + y


def _outproj(mix_ab, mix_c, x, w_out_bf, layer, tm):
    m = x.shape[0]
    half = W_CONV + W_POOL
    return pl.pallas_call(
        _outproj_kernel,
        out_shape=jax.ShapeDtypeStruct((m, D_MODEL), F32),
        grid=(m // tm, D_MODEL // COL),
        in_specs=[pl.BlockSpec((tm, half), lambda i, j: (i, 0)),
                  pl.BlockSpec((tm, W_ATTN), lambda i, j: (i, 0)),
                  pl.BlockSpec((tm, COL), lambda i, j: (i, j)),
                  pl.BlockSpec((None, half, COL), lambda i, j: (layer, 0, j)),
                  pl.BlockSpec((None, W_ATTN, COL), lambda i, j: (layer, 1, j))],
        out_specs=pl.BlockSpec((tm, COL), lambda i, j: (i, j)),
        scratch_shapes=[pltpu.VMEM((tm, half + W_ATTN), BF16)],
        compiler_params=pltpu.CompilerParams(
            dimension_semantics=("parallel", "arbitrary"), vmem_limit_bytes=VMEM_LIMIT),
        name="outproj",
    )(mix_ab, mix_c, x, w_out_bf, w_out_bf)


def _rope_tables(pos):
    half = HEAD_DIM // 2
    inv = ROPE_THETA ** (-jnp.arange(half, dtype=F32) / half)
    ang = pos.astype(F32)[:, None] * inv[None, :]
    cos, sin = jnp.cos(ang), jnp.sin(ang)
    return jnp.concatenate([cos, cos], axis=-1), jnp.concatenate([-sin, sin], axis=-1)


def kernel(x_prompt, x_sample, cache_k, cache_v, state_conv, state_pool, page_table, norm_g, w_in, w_out,
           conv_w, conv_b, conv_ln_g, conv_ln_b, pool_w, pool_scale, q_norm_g, k_norm_g):
    batch, seq, _ = x_prompt.shape
    db, dec_seq, _ = x_sample.shape
    depth = w_in.shape[0]
    past_len = page_table.shape[1] * PAGE_SIZE

    hp = x_prompt.reshape(batch * seq, D_MODEL)
    hs = x_sample.reshape(db * dec_seq, D_MODEL)
    cos_p, sin_p = _rope_tables(jnp.arange(seq, dtype=jnp.int32))
    cos_s, sin_s = _rope_tables(past_len + jnp.arange(db * dec_seq, dtype=jnp.int32) % dec_seq)
    pt_flat = page_table.reshape(-1)
    zero_cctx = jnp.zeros((batch, CONV_CTX, W_CONV), F32)
    zero_pctx = jnp.zeros((batch, POOL_CTX, W_POOL), F32)
    w_in_bf = w_in.astype(BF16)
    w_out_bf = w_out.astype(BF16)

    states = [[] for _ in range(4)]
    kv_p = kv_s = None
    for l in range(depth):
        pool_w_bf = pool_w[l].astype(BF16)
        small = (conv_w[l], conv_b[l][None], conv_ln_g[l][None], conv_ln_b[l][None], pool_w_bf, pool_scale[l][None])
        ng, qg, kg = norm_g[l][None], q_norm_g[l][None], k_norm_g[l][None]

        z, *kv_p = _proj(hp, ng, w_in_bf, cos_p, sin_p, qg, kg, tm=1024, layer=l, depth=depth, kv_fin=kv_p)
        mix_ab, c_new, p_new = _convpool(z.reshape(batch, seq, D_IN), zero_cctx, zero_pctx, *small,
                                         tt=256, rows=64, pos0=0)
        mix_c = _moba(z, batch, seq)
        hp = _outproj(mix_ab.reshape(batch * seq, -1), mix_c, hp, w_out_bf, l, tm=1024)
        states[0].append(c_new)
        states[1].append(p_new)

        z, *kv_s = _proj(hs, ng, w_in_bf, cos_s, sin_s, qg, kg, tm=db * dec_seq, layer=l, depth=depth, kv_fin=kv_s)
        z = z.reshape(db, dec_seq, D_IN)
        mix_ab, c_new, p_new = _convpool(z, state_conv[l], state_pool[l], *small,
                                         tt=dec_seq, rows=dec_seq, pos0=past_len)
        q3, k3, v3, g3 = (z[..., j0 * COL:j0 * COL + W_ATTN] for j0 in (J_Q, J_K, J_V, J_G))
        mix_c = _dec_attention(pt_flat, cache_k, cache_v, l, q3, k3, v3, g3)
        hs = _outproj(mix_ab.reshape(db * dec_seq, -1), mix_c.reshape(db * dec_seq, W_ATTN), hs, w_out_bf,
                      l, tm=db * dec_seq)
        states[2].append(c_new)
        states[3].append(p_new)

    kv_shape_p = (depth, batch, seq, N_HEADS, HEAD_DIM)
    kv_shape_s = (depth, db, dec_seq, N_HEADS, HEAD_DIM)
    return (hp.reshape(batch, seq, D_MODEL), hs.reshape(db, dec_seq, D_MODEL),
            kv_p[0].reshape(kv_shape_p), kv_p[1].reshape(kv_shape_p),
            jnp.stack(states[0]), jnp.stack(states[1]),
            kv_s[0].reshape(kv_shape_s), kv_s[1].reshape(kv_shape_s),
            jnp.stack(states[2]), jnp.stack(states[3]))
```

```python
import functools
import math

import jax
import jax.numpy as jnp
from jax import lax
from jax.experimental import pallas as pl
from jax.experimental.pallas import tpu as pltpu

F32 = jnp.float32
BF16 = jnp.bfloat16

D_MODEL = 2048
N_HEADS = 8
HEAD_DIM = 128
W_ATTN = N_HEADS * HEAD_DIM
W_CONV = 512
W_POOL = 512
CONV_WIDTH = 31
CONV_CTX = CONV_WIDTH - 1
POOL_WINDOWS = (2, 4, 8, 16)
POOL_GROUP = W_POOL // len(POOL_WINDOWS)
POOL_CTX = max(POOL_WINDOWS) - 1
MOBA_BLOCK = 256
MOBA_TOPK = 3
PAGE_SIZE = 128
ROPE_THETA = 10000.0
EPS = 1e-6
W_MISC = 3 * W_CONV + 2 * W_POOL
D_IN = W_MISC + 4 * W_ATTN
ATTN_SCALE = HEAD_DIM ** -0.5
LOG2E = math.log2(math.e)

COL = 512
COL_HEADS = COL // HEAD_DIM
N_COL = D_IN // COL
J_Q, J_K, J_V, J_G = 5, 7, 9, 11
CONV_PAD = 32
POOL_PAD = 16
PROJ_CHUNK = 256
MOBA_HEADS = 4
PAGES_PER_STEP = 32
VMEM_LIMIT = 56 * 1024 * 1024

_NT = (((1,), (1,)), ((), ()))


def _silu(x):
    return x * jax.nn.sigmoid(x)


def _norm_rope(z, gain, cos, sin):
    outs = []
    for hh in range(COL_HEADS):
        xh = z[:, hh * HEAD_DIM:(hh + 1) * HEAD_DIM]
        ms = jnp.mean(xh * xh, axis=-1, keepdims=True)
        y = xh * lax.rsqrt(ms + EPS) * gain
        outs.append(y * cos + pltpu.roll(y, HEAD_DIM // 2, 1) * sin)
    return jnp.concatenate(outs, axis=-1)


def _store_heads(fin_ref, tile, first_row):
    rows = tile.shape[0]
    for hh in range(COL_HEADS):
        fin_ref[pl.ds(first_row + hh, rows, stride=N_HEADS), :] = tile[:, hh * HEAD_DIM:(hh + 1) * HEAD_DIM]


def _proj_kernel(*refs, aliased, chunk):
    x_ref, ng_ref, w_ref, cos_ref, sin_ref, qg_ref, kg_ref = refs[:7]
    z_ref, kfin_ref, vfin_ref, h_scr = refs[7 + 2 * aliased:]
    j = pl.program_id(1)
    tm = x_ref.shape[0]

    @pl.when(j == 0)
    def _():
        x = x_ref[...]
        ms = jnp.mean(x * x, axis=-1, keepdims=True)
        h_scr[...] = (x * lax.rsqrt(ms + EPS) * ng_ref[...]).astype(BF16)

    def columns(epilogue):
        w = w_ref[...].astype(BF16)
        for r0 in range(0, tm, chunk):
            rows = slice(r0, r0 + chunk)
            z_ref[rows, :] = epilogue(jnp.dot(h_scr[rows, :], w, preferred_element_type=F32), r0)

    def rope_with(gain_ref):
        def epilogue(z, r0):
            return _norm_rope(z, gain_ref[...], cos_ref[r0:r0 + chunk, :], sin_ref[r0:r0 + chunk, :])
        return epilogue

    def copy_to(fin_ref, j0, inner=lambda z, r0: z):
        def epilogue(z, r0):
            z = inner(z, r0)
            _store_heads(fin_ref, z, r0 * N_HEADS + (j - j0) * COL_HEADS)
            return z
        return epilogue

    @pl.when((j < J_Q) | (j >= J_G))
    def _():
        columns(lambda z, r0: z)

    @pl.when((j >= J_Q) & (j < J_K))
    def _():
        columns(rope_with(qg_ref))

    @pl.when((j >= J_K) & (j < J_V))
    def _():
        columns(copy_to(kfin_ref, J_K, rope_with(kg_ref)))

    @pl.when((j >= J_V) & (j < J_G))
    def _():
        columns(copy_to(vfin_ref, J_V))


def _proj(x, norm_g, w_in, cos, sin, qg, kg, tm, layer=0, depth=1, kv_fin=None):
    m = x.shape[0]
    n_pos_tiles = cos.shape[0] // tm
    aliased = kv_fin is not None
    fin_shape = jax.ShapeDtypeStruct((depth, m * N_HEADS, HEAD_DIM), F32)
    fin_spec = pl.BlockSpec((None, tm * N_HEADS, HEAD_DIM), lambda i, j: (layer, i, 0))
    in_specs = [
        pl.BlockSpec((tm, D_MODEL), lambda i, j: (i, 0)),
        pl.BlockSpec((1, D_MODEL), lambda i, j: (0, 0)),
        pl.BlockSpec((None, D_MODEL, COL), lambda i, j: (layer, 0, j)),
        pl.BlockSpec((tm, HEAD_DIM), lambda i, j: (i % n_pos_tiles, 0)),
        pl.BlockSpec((tm, HEAD_DIM), lambda i, j: (i % n_pos_tiles, 0)),
        pl.BlockSpec((1, HEAD_DIM), lambda i, j: (0, 0)),
        pl.BlockSpec((1, HEAD_DIM), lambda i, j: (0, 0)),
    ]
    args = [x, norm_g, w_in, cos, sin, qg, kg]
    aliases = {}
    if aliased:
        in_specs += [pl.BlockSpec(memory_space=pl.ANY), pl.BlockSpec(memory_space=pl.ANY)]
        args += list(kv_fin)
        aliases = {7: 1, 8: 2}
    return pl.pallas_call(
        functools.partial(_proj_kernel, aliased=aliased, chunk=min(tm, PROJ_CHUNK)),
        out_shape=(jax.ShapeDtypeStruct((m, D_IN), F32), fin_shape, fin_shape),
        grid=(m // tm, N_COL),
        in_specs=in_specs,
        out_specs=(pl.BlockSpec((tm, COL), lambda i, j: (i, j)), fin_spec, fin_spec),
        scratch_shapes=[pltpu.VMEM((tm, D_MODEL), BF16)],
        input_output_aliases=aliases,
        compiler_params=pltpu.CompilerParams(
            dimension_semantics=("parallel", "arbitrary"), vmem_limit_bytes=VMEM_LIMIT),
        name="proj",
    )(*args)


def _causal_conv(cs, cw_ref, cb_ref, base, rows, lane_group):
    lanes = slice(lane_group * HEAD_DIM, (lane_group + 1) * HEAD_DIM)
    acc = jnp.broadcast_to(cb_ref[:, lanes], (rows, HEAD_DIM))
    first = CONV_PAD - CONV_CTX
    for r in range(8):
        part = None
        for d in range(r, CONV_PAD + 1, 8):
            if d < first:
                continue
            term = cw_ref[d - first:d - first + 1, lanes] * cs[base + d - r:base + d - r + rows + 8, lanes]
            part = term if part is None else part + term
        acc = acc + part[r:r + rows]
    return acc


def _convpool_kernel(a_ref, b_ref, ga_ref, u_ref, gb_ref, cctx_ref, pctx_ref,
                     cw_ref, cb_ref, lg_ref, lb_ref, pw_ref, ps_ref,
                     mix_ref, cnew_ref, pnew_ref, cs, ps, *, tt, rows, n_t, pos0):
    t = pl.program_id(1)

    @pl.when(t == 0)
    def _():
        cs[0:CONV_PAD - CONV_CTX, :] = jnp.zeros((CONV_PAD - CONV_CTX, W_CONV), F32)
        cs[CONV_PAD - CONV_CTX:CONV_PAD, :] = cctx_ref[0]
        cs[CONV_PAD + tt:CONV_PAD + tt + 8, :] = jnp.zeros((8, W_CONV), F32)
        ps[0:POOL_PAD - POOL_CTX, :] = jnp.zeros((POOL_PAD - POOL_CTX, W_POOL), F32)
        ps[POOL_PAD - POOL_CTX:POOL_PAD, :] = pctx_ref[0]

    cs[CONV_PAD:CONV_PAD + tt, :] = a_ref[0] * jax.nn.sigmoid(b_ref[0])
    ps[POOL_PAD:POOL_PAD + tt, :] = u_ref[0]

    for base in range(0, tt, rows):
        acc = jnp.concatenate([_causal_conv(cs, cw_ref, cb_ref, base, rows, lg)
                               for lg in range(W_CONV // HEAD_DIM)], axis=-1)
        mu =jnp.mean(acc, axis=-1, keepdims=True)
        d = acc - mu
        var = jnp.mean(d * d, axis=-1, keepdims=True)
        ya = _silu(d * lax.rsqrt(var + EPS) * lg_ref[...] + lb_ref[...])
        mix_ref[0, pl.ds(base, rows), 0:W_CONV] = (
            ya * _silu(ga_ref[0, pl.ds(base, rows), :])).astype(mix_ref.dtype)
        pos = pos0 + t * tt + base + lax.broadcasted_iota(jnp.int32, (rows, 1), 0)
        for g, w in enumerate(POOL_WINDOWS):
            sl = slice(g * POOL_GROUP, (g + 1) * POOL_GROUP)
            tok = ps[pl.ds(base + POOL_PAD, rows), sl]
            s = tok
            for i in range(1, w):
                s = s + ps[pl.ds(base + POOL_PAD - i, rows), sl]
            cnt = jnp.minimum(w, pos + 1).astype(F32)
            p = s / cnt - tok
            yb = jnp.dot(p.astype(BF16), pw_ref[g], preferred_element_type=F32) * ps_ref[:, sl]
            mix_ref[0, pl.ds(base, rows), W_CONV + g * POOL_GROUP:W_CONV + (g + 1) * POOL_GROUP] = (
                yb * _silu(gb_ref[0, pl.ds(base, rows), sl])).astype(mix_ref.dtype)

    @pl.when(t == n_t - 1)
    def _():
        cnew_ref[0] = cs[tt + CONV_PAD - CONV_CTX:tt + CONV_PAD, :]
        pnew_ref[0] = ps[tt + POOL_PAD - POOL_CTX:tt + POOL_PAD, :]

    if n_t > 1:
        @pl.when(t < n_t - 1)
        def _():
            cs[0:CONV_PAD, :] = cs[tt:tt + CONV_PAD, :]
            ps[0:POOL_PAD, :] = ps[tt:tt + POOL_PAD, :]


def _convpool(misc, cctx, pctx, conv_w, conv_b, ln_g, ln_b, pool_w_bf, pool_scale, tt, rows, pos0, mix_dtype=F32):
    n, t_len, _ = misc.shape
    n_t = t_len // tt

    def col(c):
        return pl.BlockSpec((1, tt, COL), lambda b, t: (b, t, c))

    def full(shape):
        return pl.BlockSpec(shape, lambda b, t: (0,) * len(shape))

    return pl.pallas_call(
        functools.partial(_convpool_kernel, tt=tt, rows=rows, n_t=n_t, pos0=pos0),
        out_shape=(jax.ShapeDtypeStruct((n, t_len, W_CONV + W_POOL), mix_dtype),
                   jax.ShapeDtypeStruct((n, CONV_CTX, W_CONV), F32),
                   jax.ShapeDtypeStruct((n, POOL_CTX, W_POOL), F32)),
        grid=(n, n_t),
        in_specs=[col(0), col(1), col(2), col(3), col(4),
                  pl.BlockSpec((1, CONV_CTX, W_CONV), lambda b, t: (b, 0, 0)),
                  pl.BlockSpec((1, POOL_CTX, W_POOL), lambda b, t: (b, 0, 0)),
                  full((CONV_WIDTH, W_CONV)), full((1, W_CONV)), full((1, W_CONV)), full((1, W_CONV)),
                  full((len(POOL_WINDOWS), POOL_GROUP, POOL_GROUP)), full((1, W_POOL))],
        out_specs=(pl.BlockSpec((1, tt, W_CONV + W_POOL), lambda b, t: (b, t, 0)),
                   pl.BlockSpec((1, CONV_CTX, W_CONV), lambda b, t: (b, 0, 0)),
                   pl.BlockSpec((1, POOL_CTX, W_POOL), lambda b, t: (b, 0, 0))),
        scratch_shapes=[pltpu.VMEM((CONV_PAD + tt + 8, W_CONV), F32),
                        pltpu.VMEM((POOL_PAD + tt, W_POOL), F32)],
        compiler_params=pltpu.CompilerParams(
            dimension_semantics=("parallel", "arbitrary"), vmem_limit_bytes=VMEM_LIMIT),
        name="convpool",
    )(misc, misc, misc, misc, misc, cctx, pctx, conv_w, conv_b, ln_g, ln_b, pool_w_bf, pool_scale)


def _top_blocks(gate, idx, n_valid, axis):
    n_blocks = gate.shape[axis]
    gate = jnp.where(idx < n_valid, gate, -jnp.inf)
    sel = jnp.zeros(gate.shape, F32)
    for _ in range(MOBA_TOPK):
        best = jnp.max(gate, axis=axis, keepdims=True)
        first = jnp.min(jnp.where(gate == best, idx, n_blocks), axis=axis, keepdims=True)
        pick = idx == first
        sel = jnp.where(pick, 1.0, sel)
        gate = jnp.where(pick, -jnp.inf, gate)
    return jnp.where(idx < n_valid, sel, 0.0)


def _moba_kernel(q_ref, k_ref, v_ref, g_ref, o_ref, kbf, vtbf, kmean, sel_scr, acc_scr, qs_scr, s_even, s_odd,
                 *, n_blocks):
    c = pl.program_id(2)

    @pl.when(c == 0)
    def _():
        def prep(n, carry):
            rows = pl.ds(pl.multiple_of(n * MOBA_BLOCK, MOBA_BLOCK), MOBA_BLOCK)
            for hh in range(MOBA_HEADS):
                lanes = slice(hh * HEAD_DIM, (hh + 1) * HEAD_DIM)
                kb = k_ref[rows, lanes]
                kbf[hh, n] = kb.astype(BF16)
                kmean[hh, pl.ds(n, 1), :] = jnp.sum(kb, axis=0, keepdims=True) * (1.0 / MOBA_BLOCK)
                vtbf[hh, n] = v_ref[rows, lanes].T.astype(BF16)
            return carry
        lax.fori_loop(0, n_blocks, prep, 0)

    qf = [q_ref[:, hh * HEAD_DIM:(hh + 1) * HEAD_DIM] for hh in range(MOBA_HEADS)]
    for hh in range(MOBA_HEADS):
        qs_scr[hh] = (qf[hh] * (ATTN_SCALE * LOG2E)).astype(BF16)
    scores = [lax.dot_general(kbf[hh, c], qs_scr[hh], _NT, preferred_element_type=F32)
              for hh in range(MOBA_HEADS)]
    gates = [lax.dot_general(kmean[hh], qf[hh], _NT, precision=lax.Precision.HIGHEST,
                             preferred_element_type=F32) for hh in range(MOBA_HEADS)]
    kpos = lax.broadcasted_iota(jnp.int32, scores[0].shape, 0)
    qpos = lax.broadcasted_iota(jnp.int32, scores[0].shape, 1)
    blk = lax.broadcasted_iota(jnp.int32, gates[0].shape, 0)
    stats = []
    for hh in range(MOBA_HEADS):
        s = jnp.where(kpos <= qpos, scores[hh], -jnp.inf)
        m0 = jnp.max(s, axis=0, keepdims=True)
        p = jnp.exp2(s - m0)
        stats.append((m0, jnp.sum(p, axis=0, keepdims=True)))
        acc_scr[hh] = jnp.dot(vtbf[hh, c], p.astype(BF16), preferred_element_type=F32)
    for hh in range(MOBA_HEADS):
        sel_scr[hh] = _top_blocks(gates[hh], blk, c, 0)

    def score(buf, n):
        for hh in range(MOBA_HEADS):
            buf[hh] = lax.dot_general(kbf[hh, n], qs_scr[hh], _NT, preferred_element_type=F32)

    def reduce(buf, n, carry):
        new = []
        for hh in range(MOBA_HEADS):
            m, l = carry[hh]
            s = jnp.where(sel_scr[hh, pl.ds(n, 1), :] > 0.0, buf[hh], -jnp.inf)
            m_new = jnp.maximum(m, jnp.max(s, axis=0, keepdims=True))
            alpha = jnp.exp2(m - m_new)
            p = jnp.exp2(s - m_new)
            acc_scr[hh] = alpha * acc_scr[hh] + jnp.dot(vtbf[hh, n], p.astype(BF16), preferred_element_type=F32)
            new.append((m_new, alpha * l + jnp.sum(p, axis=0, keepdims=True)))
        return tuple(new)

    def past(i, carry):
        n = 2 * i
        score(s_odd, n + 1)
        carry = reduce(s_even, n, carry)
        score(s_even, jnp.minimum(n + 2, n_blocks - 1))
        return reduce(s_odd, n + 1, carry)

    score(s_even, 0)
    stats = lax.fori_loop(0, (c + 1) // 2, past, tuple(stats))
    for hh in range(MOBA_HEADS):
        lanes = slice(hh * HEAD_DIM, (hh + 1) * HEAD_DIM)
        o_ref[:, lanes] = ((acc_scr[hh] / stats[hh][1]).T * _silu(g_ref[:, lanes])).astype(o_ref.dtype)


def _moba(z, batch, t_len):
    assert MOBA_HEADS == COL_HEADS
    m = z.shape[0]
    n_blocks = t_len // MOBA_BLOCK

    def rows(j0):
        return pl.BlockSpec((MOBA_BLOCK, COL), lambda b, h, c: (b * n_blocks + c, j0 + h))

    def seq(j0):
        return pl.BlockSpec((t_len, COL), lambda b, h, c: (b, j0 + h))

    return pl.pallas_call(
        functools.partial(_moba_kernel, n_blocks=n_blocks),
        out_shape=jax.ShapeDtypeStruct((m, W_ATTN), BF16),
        grid=(batch, N_HEADS // MOBA_HEADS, n_blocks),
        in_specs=[rows(J_Q), seq(J_K), seq(J_V), rows(J_G)],
        out_specs=rows(0),
        scratch_shapes=[pltpu.VMEM((MOBA_HEADS, n_blocks, MOBA_BLOCK, HEAD_DIM), BF16),
                        pltpu.VMEM((MOBA_HEADS, n_blocks, HEAD_DIM, MOBA_BLOCK), BF16),
                        pltpu.VMEM((MOBA_HEADS, n_blocks, HEAD_DIM), F32),
                        pltpu.VMEM((MOBA_HEADS, n_blocks, MOBA_BLOCK), F32),
                        pltpu.VMEM((MOBA_HEADS, HEAD_DIM, MOBA_BLOCK), F32),
                        pltpu.VMEM((MOBA_HEADS, MOBA_BLOCK, HEAD_DIM), BF16),
                        pltpu.VMEM((MOBA_HEADS, MOBA_BLOCK, MOBA_BLOCK), F32),
                        pltpu.VMEM((MOBA_HEADS, MOBA_BLOCK, MOBA_BLOCK), F32)],
        compiler_params=pltpu.CompilerParams(
            dimension_semantics=("parallel", "parallel", "arbitrary"), vmem_limit_bytes=VMEM_LIMIT),
        name="moba",
    )(z, z, z, z)


def _head_rows(x4):
    rows = jnp.concatenate(
        [jnp.broadcast_to(x4[s:s + 1, :], (N_HEADS, W_ATTN)) for s in range(x4.shape[0])], axis=0)
    row_h = lax.broadcasted_iota(jnp.int32, rows.shape, 0) % N_HEADS
    lane_h = lax.broadcasted_iota(jnp.int32, rows.shape, 1) // HEAD_DIM
    return jnp.where(row_h == lane_h, rows, 0.0)


def _gather_heads(ref, n_rows):
    return jnp.concatenate([ref[pl.ds(h, n_rows, stride=N_HEADS), :] for h in range(N_HEADS)], axis=-1)


def _dec_qk_kernel(pt_ref, *refs, n_steps, dec_seq):
    pages = refs[:PAGES_PER_STEP]
    q_ref, kn_ref, vn_ref, p_ref, own_ref, kmean, wq = refs[PAGES_PER_STEP:]
    st = pl.program_id(1)
    n_rows = dec_seq * N_HEADS
    n_pages = n_steps * PAGES_PER_STEP
    pages_per_block = MOBA_BLOCK // PAGE_SIZE
    blocks_per_step = PAGES_PER_STEP // pages_per_block
    n_blocks = n_pages // pages_per_block

    @pl.when(st == 0)
    def _():
        wq[...] = _head_rows(q_ref[0])

    w = wq[...]
    for i in range(PAGES_PER_STEP):
        page = _gather_heads(pages[i], PAGE_SIZE)
        pagesum = jnp.sum(page.reshape(PAGE_SIZE // 8, 8, W_ATTN), axis=0)
        blocksum = pagesum if i % pages_per_block == 0 else blocksum + pagesum
        if i % pages_per_block == pages_per_block - 1:
            kmean[pl.ds(st * blocks_per_step + i // pages_per_block, 1), :] = (
                jnp.sum(blocksum, axis=0, keepdims=True) * (1.0 / MOBA_BLOCK))
        p_ref[0, st * PAGES_PER_STEP + i] = (
            lax.dot_general(w, page, _NT, preferred_element_type=F32) * ATTN_SCALE)

    @pl.when(st == n_steps - 1)
    def _():
        gate = lax.dot_general(w, kmean[...], _NT, precision=lax.Precision.HIGHEST,
                               preferred_element_type=F32)
        blk = lax.broadcasted_iota(jnp.int32, gate.shape, 1)
        sel = _top_blocks(gate, blk, n_blocks, 1)
        lo = lax.dot_general(w, kn_ref[0], _NT, preferred_element_type=F32) * ATTN_SCALE
        tok = lax.broadcasted_iota(jnp.int32, lo.shape, 0) // N_HEADS
        key = lax.broadcasted_iota(jnp.int32, lo.shape, 1)
        lo = jnp.where(key <= tok, lo, -jnp.inf)
        m = jnp.max(lo, axis=1, keepdims=True)
        mp = jnp.full((n_rows, PAGE_SIZE), -jnp.inf, F32)
        for pg in range(n_pages):
            keep = sel[:, pg // pages_per_block:pg // pages_per_block + 1] > 0.0
            mp = jnp.maximum(mp, jnp.where(keep, p_ref[0, pg], -jnp.inf))
        m = jnp.maximum(m, jnp.max(mp, axis=1, keepdims=True))
        p_own = jnp.exp(lo - m)
        lsum = jnp.zeros((n_rows, PAGE_SIZE), F32)
        for pg in range(n_pages):
            keep = sel[:, pg // pages_per_block:pg // pages_per_block + 1] > 0.0
            e = jnp.where(keep, jnp.exp(p_ref[0, pg] - m), 0.0)
            p_ref[0, pg] = e
            lsum = lsum + e
        inv = 1.0 / (jnp.sum(lsum, axis=1, keepdims=True) + jnp.sum(p_own, axis=1, keepdims=True))
        for pg in range(n_pages):
            p_ref[0, pg] = p_ref[0, pg] * inv
        p_own = p_own * inv
        vn = vn_ref[0]
        own = p_own[:, 0:1] * vn[0:1, :]
        for jj in range(1, dec_seq):
            own = own + p_own[:, jj:jj + 1] * vn[jj:jj + 1, :]
        own_ref[0] = own


def _page_spec(layer, i, n_pages):
    return pl.BlockSpec(
        (None, None, PAGE_SIZE * N_HEADS, HEAD_DIM),
        lambda b, st, pt: (layer, pt[b * n_pages + st * PAGES_PER_STEP + i], 0, 0))


def _dec_qk(pt_flat, cache_k, layer, q, kn, vn):
    db, dec_seq, _ = q.shape
    n_pages = pt_flat.shape[0] // db
    n_steps = n_pages // PAGES_PER_STEP
    n_rows = dec_seq * N_HEADS
    n_blocks = n_pages * PAGE_SIZE // MOBA_BLOCK
    tok = pl.BlockSpec((1, dec_seq, W_ATTN), lambda b, st, pt: (b, 0, 0))
    grid_spec = pltpu.PrefetchScalarGridSpec(
        num_scalar_prefetch=1,
        grid=(db, n_steps),
        in_specs=[_page_spec(layer, i, n_pages) for i in range(PAGES_PER_STEP)] + [tok, tok, tok],
        out_specs=(pl.BlockSpec((1, n_pages, n_rows, PAGE_SIZE), lambda b, st, pt: (b, 0, 0, 0)),
                   pl.BlockSpec((1, n_rows, W_ATTN), lambda b, st, pt: (b, 0, 0))),
        scratch_shapes=[pltpu.VMEM((n_blocks, W_ATTN), F32),
                        pltpu.VMEM((n_rows, W_ATTN), F32)],
    )
    return pl.pallas_call(
        functools.partial(_dec_qk_kernel, n_steps=n_steps, dec_seq=dec_seq),
        out_shape=(jax.ShapeDtypeStruct((db, n_pages, n_rows, PAGE_SIZE), F32),
                   jax.ShapeDtypeStruct((db, n_rows, W_ATTN), F32)),
        grid_spec=grid_spec,
        compiler_params=pltpu.CompilerParams(
            dimension_semantics=("parallel", "arbitrary"), vmem_limit_bytes=VMEM_LIMIT),
        name="dec_qk",
    )(pt_flat, *([cache_k] * PAGES_PER_STEP), q, kn, vn)


def _dec_pv_kernel(pt_ref, *refs, n_steps, dec_seq):
    pages = refs[:PAGES_PER_STEP]
    p_ref, own_ref, g_ref, o_ref, acc = refs[PAGES_PER_STEP:]
    st = pl.program_id(1)

    @pl.when(st == 0)
    def _():
        acc[...] = own_ref[0]

    a = acc[...]
    for i in range(PAGES_PER_STEP):
        a = a + jnp.dot(p_ref[0, st * PAGES_PER_STEP + i], _gather_heads(pages[i], PAGE_SIZE),
                        preferred_element_type=F32)
    acc[...] = a

    @pl.when(st == n_steps - 1)
    def _():
        row_h = lax.broadcasted_iota(jnp.int32, a.shape, 0) % N_HEADS
        lane_h = lax.broadcasted_iota(jnp.int32, a.shape, 1) // HEAD_DIM
        diag = jnp.where(row_h == lane_h, a, 0.0)
        heads = jnp.concatenate(
            [jnp.sum(diag[s * N_HEADS:(s + 1) * N_HEADS], axis=0, keepdims=True) for s in range(dec_seq)], axis=0)
        o_ref[0] = heads * _silu(g_ref[0])


def _dec_pv(pt_flat, cache_v, layer, probs, own, g):
    db, n_pages, n_rows, _ = probs.shape
    dec_seq = n_rows // N_HEADS
    n_steps = n_pages // PAGES_PER_STEP
    tok = pl.BlockSpec((1, dec_seq, W_ATTN), lambda b, st, pt: (b, 0, 0))
    grid_spec = pltpu.PrefetchScalarGridSpec(
        num_scalar_prefetch=1,
        grid=(db, n_steps),
        in_specs=[_page_spec(layer, i, n_pages) for i in range(PAGES_PER_STEP)] + [
            pl.BlockSpec((1, n_pages, n_rows, PAGE_SIZE), lambda b, st, pt: (b, 0, 0, 0)),
            pl.BlockSpec((1, n_rows, W_ATTN), lambda b, st, pt: (b, 0, 0)),
            tok],
        out_specs=tok,
        scratch_shapes=[pltpu.VMEM((n_rows, W_ATTN), F32)],
    )
    return pl.pallas_call(
        functools.partial(_dec_pv_kernel, n_steps=n_steps, dec_seq=dec_seq),
        out_shape=jax.ShapeDtypeStruct((db, dec_seq, W_ATTN), F32),
        grid_spec=grid_spec,
        compiler_params=pltpu.CompilerParams(
            dimension_semantics=("parallel", "arbitrary"), vmem_limit_bytes=VMEM_LIMIT),
        name="dec_pv",
    )(pt_flat, *([cache_v] * PAGES_PER_STEP), probs, own, g)


def _dec_attention(pt_flat, cache_k, cache_v, layer, q, kn, vn, g):
    depth, n_phys = cache_k.shape[:2]
    view = (depth, n_phys, PAGE_SIZE * N_HEADS, HEAD_DIM)
    probs, own = _dec_qk(pt_flat, cache_k.reshape(view), layer, q, kn, vn)
    return _dec_pv(pt_flat, cache_v.reshape(view), layer, probs, own, g)


def _outproj_kernel(ab_ref, c_ref, x_ref, wa_ref, wc_ref, y_ref):
    y = jnp.dot(ab_ref[...].astype(BF16), wa_ref[...], preferred_element_type=F32)
    y = y + jnp.dot(c_ref[...].astype(BF16), wc_ref[...], preferred_element_type=F32)
    y_ref[...] = x_ref[...] + y


def _outproj(mix_ab, mix_c, x, w_out_bf, layer, tm):
    m = x.shape[0]
    half = W_CONV + W_POOL
    return pl.pallas_call(
        _outproj_kernel,
        out_shape=jax.ShapeDtypeStruct((m, D_MODEL), F32),
        grid=(m // tm, D_MODEL // COL),
        in_specs=[pl.BlockSpec((tm, half), lambda i, j: (i, 0)),
                  pl.BlockSpec((tm, W_ATTN), lambda i, j: (i, 0)),
                  pl.BlockSpec((tm, COL), lambda i, j: (i, j)),
                  pl.BlockSpec((None, half, COL), lambda i, j: (layer, 0, j)),
                  pl.BlockSpec((None, W_ATTN, COL), lambda i, j: (layer, 1, j))],
        out_specs=pl.BlockSpec((tm, COL), lambda i, j: (i, j)),
        compiler_params=pltpu.CompilerParams(
            dimension_semantics=("parallel", "arbitrary"), vmem_limit_bytes=VMEM_LIMIT),
        name="outproj",
    )(mix_ab, mix_c, x, w_out_bf, w_out_bf)


def _rope_tables(pos):
    half = HEAD_DIM // 2
    inv = ROPE_THETA ** (-jnp.arange(half, dtype=F32) / half)
    ang = pos.astype(F32)[:, None] * inv[None, :]
    cos, sin = jnp.cos(ang), jnp.sin(ang)
    return jnp.concatenate([cos, cos], axis=-1), jnp.concatenate([-sin, sin], axis=-1)


def kernel(x_prompt, x_sample, cache_k, cache_v, state_conv, state_pool, page_table, norm_g, w_in, w_out,
           conv_w, conv_b, conv_ln_g, conv_ln_b, pool_w, pool_scale, q_norm_g, k_norm_g):
    batch, seq, _ = x_prompt.shape
    db, dec_seq, _ = x_sample.shape
    depth = w_in.shape[0]
    past_len = page_table.shape[1] * PAGE_SIZE

    hp = x_prompt.reshape(batch * seq, D_MODEL)
    hs = x_sample.reshape(db * dec_seq, D_MODEL)
    cos_p, sin_p = _rope_tables(jnp.arange(seq, dtype=jnp.int32))
    cos_s, sin_s = _rope_tables(past_len + jnp.arange(db * dec_seq, dtype=jnp.int32) % dec_seq)
    pt_flat = page_table.reshape(-1)
    zero_cctx = jnp.zeros((batch, CONV_CTX, W_CONV), F32)
    zero_pctx = jnp.zeros((batch, POOL_CTX, W_POOL), F32)
    w_out_bf = w_out.astype(BF16)

    states = [[] for _ in range(4)]
    kv_p = kv_s = None
    for l in range(depth):
        pool_w_bf = pool_w[l].astype(BF16)
        small = (conv_w[l], conv_b[l][None], conv_ln_g[l][None], conv_ln_b[l][None], pool_w_bf, pool_scale[l][None])
        ng, qg, kg = norm_g[l][None], q_norm_g[l][None], k_norm_g[l][None]

        z, *kv_p = _proj(hp, ng, w_in, cos_p, sin_p, qg, kg, tm=1024, layer=l, depth=depth, kv_fin=kv_p)
        mix_ab, c_new, p_new = _convpool(z.reshape(batch, seq, D_IN), zero_cctx, zero_pctx, *small,
                                         tt=256, rows=64, pos0=0, mix_dtype=BF16)
        mix_c = _moba(z, batch, seq)
        hp = _outproj(mix_ab.reshape(batch * seq, -1), mix_c, hp, w_out_bf, l, tm=1024)
        states[0].append(c_new)
        states[1].append(p_new)

        z, *kv_s = _proj(hs, ng, w_in, cos_s, sin_s, qg, kg, tm=db * dec_seq, layer=l, depth=depth, kv_fin=kv_s)
        z = z.reshape(db, dec_seq, D_IN)
        mix_ab, c_new, p_new = _convpool(z, state_conv[l], state_pool[l], *small,
                                         tt=dec_seq, rows=dec_seq, pos0=past_len)
        q3, k3, v3, g3 = (z[..., j0 * COL:j0 * COL + W_ATTN] for j0 in (J_Q, J_K, J_V, J_G))
        mix_c = _dec_attention(pt_flat, cache_k, cache_v, l, q3, k3, v3, g3)
        hs = _outproj(mix_ab.reshape(db * dec_seq, -1), mix_c.reshape(db * dec_seq, W_ATTN), hs, w_out_bf,
                      l, tm=db * dec_seq)
        states[2].append(c_new)
        states[3].append(p_new)

    kv_shape_p = (depth, batch, seq, N_HEADS, HEAD_DIM)
    kv_shape_s = (depth, db, dec_seq, N_HEADS, HEAD_DIM)
    return (hp.reshape(batch, seq, D_MODEL), hs.reshape(db, dec_seq, D_MODEL),
            kv_p[0].reshape(kv_shape_p), kv_p[1].reshape(kv_shape_p),
            jnp.stack(states[0]), jnp.stack(states[1]),
            kv_s[0].reshape(kv_shape_s), kv_s[1].reshape(kv_shape_s),
            jnp.stack(states[2]), jnp.stack(states[3]))
```

```python
import functools
import math

import jax
import jax.numpy as jnp
from jax import lax
from jax.experimental import pallas as pl
from jax.experimental.pallas import tpu as pltpu

F32 = jnp.float32
BF16 = jnp.bfloat16

D_MODEL = 2048
N_HEADS = 8
HEAD_DIM = 128
W_ATTN = N_HEADS * HEAD_DIM
W_CONV = 512
W_POOL = 512
CONV_WIDTH = 31
CONV_CTX = CONV_WIDTH - 1
POOL_WINDOWS = (2, 4, 8, 16)
POOL_GROUP = W_POOL // len(POOL_WINDOWS)
POOL_CTX = max(POOL_WINDOWS) - 1
MOBA_BLOCK = 256
MOBA_TOPK = 3
PAGE_SIZE = 128
ROPE_THETA = 10000.0
EPS = 1e-6
W_MISC = 3 * W_CONV + 2 * W_POOL
D_IN = W_MISC + 4 * W_ATTN
ATTN_SCALE = HEAD_DIM ** -0.5
LOG2E = math.log2(math.e)

COL = 512
COL_HEADS = COL // HEAD_DIM
N_COL = D_IN // COL
J_Q, J_K, J_V, J_G = 5, 7, 9, 11
CONV_PAD = 32
POOL_PAD = 16
PROJ_CHUNK = 256
MOBA_HEADS = 4
PAGES_PER_STEP = 32
DEC_SEQS_PER_STEP = 8
VMEM_LIMIT = 56 * 1024 * 1024

_NT = (((1,), (1,)), ((), ()))


def _silu(x):
    return x * jax.nn.sigmoid(x)


def _norm_rope(z, gain, cos, sin):
    outs = []
    for hh in range(COL_HEADS):
        xh = z[:, hh * HEAD_DIM:(hh + 1) * HEAD_DIM]
        ms = jnp.mean(xh * xh, axis=-1, keepdims=True)
        y = xh * lax.rsqrt(ms + EPS) * gain
        outs.append(y * cos + pltpu.roll(y, HEAD_DIM // 2, 1) * sin)
    return jnp.concatenate(outs, axis=-1)


def _store_heads(fin_ref, tile, first_row):
    rows = tile.shape[0]
    for hh in range(COL_HEADS):
        fin_ref[pl.ds(first_row + hh, rows, stride=N_HEADS), :] = tile[:, hh * HEAD_DIM:(hh + 1) * HEAD_DIM]


def _proj_kernel(*refs, aliased, chunk):
    x_ref, ng_ref, w_ref, cos_ref, sin_ref, qg_ref, kg_ref = refs[:7]
    z_ref, kfin_ref, vfin_ref, h_scr = refs[7 + 2 * aliased:]
    j = pl.program_id(1)
    tm = x_ref.shape[0]

    @pl.when(j == 0)
    def _():
        x = x_ref[...]
        ms = jnp.mean(x * x, axis=-1, keepdims=True)
        h_scr[...] = (x * lax.rsqrt(ms + EPS) * ng_ref[...]).astype(BF16)

    def columns(epilogue):
        w = w_ref[...].astype(BF16)
        for r0 in range(0, tm, chunk):
            rows = slice(r0, r0 + chunk)
            z_ref[rows, :] = epilogue(jnp.dot(h_scr[rows, :], w, preferred_element_type=F32), r0)

    def rope_with(gain_ref):
        def epilogue(z, r0):
            return _norm_rope(z, gain_ref[...], cos_ref[r0:r0 + chunk, :], sin_ref[r0:r0 + chunk, :])
        return epilogue

    def copy_to(fin_ref, j0, inner=lambda z, r0: z):
        def epilogue(z, r0):
            z = inner(z, r0)
            _store_heads(fin_ref, z, r0 * N_HEADS + (j - j0) * COL_HEADS)
            return z
        return epilogue

    @pl.when((j < J_Q) | (j >= J_G))
    def _():
        columns(lambda z, r0: z)

    @pl.when((j >= J_Q) & (j < J_K))
    def _():
        columns(rope_with(qg_ref))

    @pl.when((j >= J_K) & (j < J_V))
    def _():
        columns(copy_to(kfin_ref, J_K, rope_with(kg_ref)))

    @pl.when((j >= J_V) & (j < J_G))
    def _():
        columns(copy_to(vfin_ref, J_V))


def _proj(x, norm_g, w_in, cos, sin, qg, kg, tm, layer=0, depth=1, kv_fin=None):
    m = x.shape[0]
    n_pos_tiles = cos.shape[0] // tm
    aliased = kv_fin is not None
    fin_shape = jax.ShapeDtypeStruct((depth, m * N_HEADS, HEAD_DIM), F32)
    fin_spec = pl.BlockSpec((None, tm * N_HEADS, HEAD_DIM), lambda i, j: (layer, i, 0))
    in_specs = [
        pl.BlockSpec((tm, D_MODEL), lambda i, j: (i, 0)),
        pl.BlockSpec((1, D_MODEL), lambda i, j: (0, 0)),
        pl.BlockSpec((None, D_MODEL, COL), lambda i, j: (layer, 0, j)),
        pl.BlockSpec((tm, HEAD_DIM), lambda i, j: (i % n_pos_tiles, 0)),
        pl.BlockSpec((tm, HEAD_DIM), lambda i, j: (i % n_pos_tiles, 0)),
        pl.BlockSpec((1, HEAD_DIM), lambda i, j: (0, 0)),
        pl.BlockSpec((1, HEAD_DIM), lambda i, j: (0, 0)),
    ]
    args = [x, norm_g, w_in, cos, sin, qg, kg]
    aliases = {}
    if aliased:
        in_specs += [pl.BlockSpec(memory_space=pl.ANY), pl.BlockSpec(memory_space=pl.ANY)]
        args += list(kv_fin)
        aliases = {7: 1, 8: 2}
    return pl.pallas_call(
        functools.partial(_proj_kernel, aliased=aliased, chunk=min(tm, PROJ_CHUNK)),
        out_shape=(jax.ShapeDtypeStruct((m, D_IN), F32), fin_shape, fin_shape),
        grid=(m // tm, N_COL),
        in_specs=in_specs,
        out_specs=(pl.BlockSpec((tm, COL), lambda i, j: (i, j)), fin_spec, fin_spec),
        scratch_shapes=[pltpu.VMEM((tm, D_MODEL), BF16)],
        input_output_aliases=aliases,
        compiler_params=pltpu.CompilerParams(
            dimension_semantics=("parallel", "arbitrary"), vmem_limit_bytes=VMEM_LIMIT),
        name="proj",
    )(*args)


def _causal_conv(cs, cw_ref, cb_ref, base, rows, lane_group):
    lanes = slice(lane_group * HEAD_DIM, (lane_group + 1) * HEAD_DIM)
    acc = jnp.broadcast_to(cb_ref[:, lanes], (rows, HEAD_DIM))
    first = CONV_PAD - CONV_CTX
    for r in range(8):
        part = None
        for d in range(r, CONV_PAD + 1, 8):
            if d < first:
                continue
            term = cw_ref[d - first:d - first + 1, lanes] * cs[base + d - r:base + d - r + rows + 8, lanes]
            part = term if part is None else part + term
        acc = acc + part[r:r + rows]
    return acc


def _convpool_kernel(*refs, nb, **static):
    seq_in, shared, seq_out = refs[:7], refs[7:13], refs[13:]
    for bi in range(nb):
        _convpool_sequence(*(r.at[bi] for r in seq_in), *shared, *(r.at[bi] for r in seq_out), **static)


def _convpool_sequence(a_ref, b_ref, ga_ref, u_ref, gb_ref, cctx_ref, pctx_ref,
                       cw_ref, cb_ref, lg_ref, lb_ref, pw_ref, ps_ref,
                       mix_ref, cnew_ref, pnew_ref, cs, ps, *, tt, rows, n_t, pos0):
    t = pl.program_id(1)

    @pl.when(t == 0)
    def _():
        cs[0:CONV_PAD - CONV_CTX, :] = jnp.zeros((CONV_PAD - CONV_CTX, W_CONV), F32)
        cs[CONV_PAD - CONV_CTX:CONV_PAD, :] = cctx_ref[...]
        cs[CONV_PAD + tt:CONV_PAD + tt + 8, :] = jnp.zeros((8, W_CONV), F32)
        ps[0:POOL_PAD - POOL_CTX, :] = jnp.zeros((POOL_PAD - POOL_CTX, W_POOL), F32)
        ps[POOL_PAD - POOL_CTX:POOL_PAD, :] = pctx_ref[...]

    cs[CONV_PAD:CONV_PAD + tt, :] = a_ref[...] * jax.nn.sigmoid(b_ref[...])
    ps[POOL_PAD:POOL_PAD + tt, :] = u_ref[...]

    for base in range(0, tt, rows):
        acc = jnp.concatenate([_causal_conv(cs, cw_ref, cb_ref, base, rows, lg)
                               for lg in range(W_CONV // HEAD_DIM)], axis=-1)
        mu =jnp.mean(acc, axis=-1, keepdims=True)
        d = acc - mu
        var = jnp.mean(d * d, axis=-1, keepdims=True)
        ya = _silu(d * lax.rsqrt(var + EPS) * lg_ref[...] + lb_ref[...])
        mix_ref[pl.ds(base, rows), 0:W_CONV] = (
            ya * _silu(ga_ref[pl.ds(base, rows), :])).astype(mix_ref.dtype)
        pos = pos0 + t * tt + base + lax.broadcasted_iota(jnp.int32, (rows, 1), 0)
        for g, w in enumerate(POOL_WINDOWS):
            sl = slice(g * POOL_GROUP, (g + 1) * POOL_GROUP)
            tok = ps[pl.ds(base + POOL_PAD, rows), sl]
            s = tok
            for i in range(1, w):
                s = s + ps[pl.ds(base + POOL_PAD - i, rows), sl]
            cnt = jnp.minimum(w, pos + 1).astype(F32)
            p = s / cnt - tok
            yb = jnp.dot(p.astype(BF16), pw_ref[g], preferred_element_type=F32) * ps_ref[:, sl]
            mix_ref[pl.ds(base, rows), W_CONV + g * POOL_GROUP:W_CONV + (g + 1) * POOL_GROUP] = (
                yb * _silu(gb_ref[pl.ds(base, rows), sl])).astype(mix_ref.dtype)

    @pl.when(t == n_t - 1)
    def _():
        cnew_ref[...] = cs[tt + CONV_PAD - CONV_CTX:tt + CONV_PAD, :]
        pnew_ref[...] = ps[tt + POOL_PAD - POOL_CTX:tt + POOL_PAD, :]

    if n_t > 1:
        @pl.when(t < n_t - 1)
        def _():
            cs[0:CONV_PAD, :] = cs[tt:tt + CONV_PAD, :]
            ps[0:POOL_PAD, :] = ps[tt:tt + POOL_PAD, :]


def _convpool(misc, cctx, pctx, conv_w, conv_b, ln_g, ln_b, pool_w_bf, pool_scale, tt, rows, pos0, nb=1, mix_dtype=F32):
    n, t_len, _ = misc.shape
    n_t = t_len // tt

    def col(c):
        return pl.BlockSpec((nb, tt, COL), lambda b, t: (b, t, c))

    def full(shape):
        return pl.BlockSpec(shape, lambda b, t: (0,) * len(shape))

    return pl.pallas_call(
        functools.partial(_convpool_kernel, nb=nb, tt=tt, rows=rows, n_t=n_t, pos0=pos0),
        out_shape=(jax.ShapeDtypeStruct((n, t_len, W_CONV + W_POOL), mix_dtype),
                   jax.ShapeDtypeStruct((n, CONV_CTX, W_CONV), F32),
                   jax.ShapeDtypeStruct((n, POOL_CTX, W_POOL), F32)),
        grid=(n // nb, n_t),
        in_specs=[col(0), col(1), col(2), col(3), col(4),
                  pl.BlockSpec((nb, CONV_CTX, W_CONV), lambda b, t: (b, 0, 0)),
                  pl.BlockSpec((nb, POOL_CTX, W_POOL), lambda b, t: (b, 0, 0)),
                  full((CONV_WIDTH, W_CONV)), full((1, W_CONV)), full((1, W_CONV)), full((1, W_CONV)),
                  full((len(POOL_WINDOWS), POOL_GROUP, POOL_GROUP)), full((1, W_POOL))],
        out_specs=(pl.BlockSpec((nb, tt, W_CONV + W_POOL), lambda b, t: (b, t, 0)),
                   pl.BlockSpec((nb, CONV_CTX, W_CONV), lambda b, t: (b, 0, 0)),
                   pl.BlockSpec((nb, POOL_CTX, W_POOL), lambda b, t: (b, 0, 0))),
        scratch_shapes=[pltpu.VMEM((nb, CONV_PAD + tt + 8, W_CONV), F32),
                        pltpu.VMEM((nb, POOL_PAD + tt, W_POOL), F32)],
        compiler_params=pltpu.CompilerParams(
            dimension_semantics=("parallel", "arbitrary"), vmem_limit_bytes=VMEM_LIMIT),
        name="convpool",
    )(misc, misc, misc, misc, misc, cctx, pctx, conv_w, conv_b, ln_g, ln_b, pool_w_bf, pool_scale)


def _top_blocks(gate, idx, n_valid, axis):
    n_blocks = gate.shape[axis]
    gate = jnp.where(idx < n_valid, gate, -jnp.inf)
    sel = jnp.zeros(gate.shape, F32)
    for _ in range(MOBA_TOPK):
        best = jnp.max(gate, axis=axis, keepdims=True)
        first = jnp.min(jnp.where(gate == best, idx, n_blocks), axis=axis, keepdims=True)
        pick = idx == first
        sel = jnp.where(pick, 1.0, sel)
        gate = jnp.where(pick, -jnp.inf, gate)
    return jnp.where(idx < n_valid, sel, 0.0)


def _top_blocks_by_rank(gate):
    n_blocks = gate.shape[1]
    idx = lax.broadcasted_iota(jnp.int32, gate.shape, 1)
    rank = jnp.zeros(gate.shape, F32)
    for other in range(n_blocks):
        col = gate[:, other:other + 1]
        rank = rank + jnp.where(col > gate, 1.0, 0.0)
        rank = rank + jnp.where(col == gate, jnp.where(idx > other, 1.0, 0.0), 0.0)
    return jnp.where(rank < MOBA_TOPK, 1.0, 0.0)


def _moba_kernel(q_ref, k_ref, v_ref, g_ref, o_ref, kbf, vtbf, kmean, sel_scr, acc_scr, qs_scr, s_even, s_odd,
                 *, n_blocks):
    c = pl.program_id(2)

    @pl.when(c == 0)
    def _():
        def prep(n, carry):
            rows = pl.ds(pl.multiple_of(n * MOBA_BLOCK, MOBA_BLOCK), MOBA_BLOCK)
            for hh in range(MOBA_HEADS):
                lanes = slice(hh * HEAD_DIM, (hh + 1) * HEAD_DIM)
                kb = k_ref[rows, lanes]
                kbf[hh, n] = kb.astype(BF16)
                kmean[hh, pl.ds(n, 1), :] = jnp.sum(kb, axis=0, keepdims=True) * (1.0 / MOBA_BLOCK)
                vtbf[hh, n] = v_ref[rows, lanes].T.astype(BF16)
            return carry
        lax.fori_loop(0, n_blocks, prep, 0)

    qf = [q_ref[:, hh * HEAD_DIM:(hh + 1) * HEAD_DIM] for hh in range(MOBA_HEADS)]
    for hh in range(MOBA_HEADS):
        qs_scr[hh] = (qf[hh] * (ATTN_SCALE * LOG2E)).astype(BF16)
    scores = [lax.dot_general(kbf[hh, c], qs_scr[hh], _NT, preferred_element_type=F32)
              for hh in range(MOBA_HEADS)]

    def score(buf, n):
        for hh in range(MOBA_HEADS):
            buf[hh] = lax.dot_general(kbf[hh, n], qs_scr[hh], _NT, preferred_element_type=F32)

    score(s_even, 0)
    gates =[lax.dot_general(kmean[hh], qf[hh], _NT, precision=lax.Precision.HIGHEST,
                             preferred_element_type=F32) for hh in range(MOBA_HEADS)]
    kpos = lax.broadcasted_iota(jnp.int32, scores[0].shape, 0)
    qpos = lax.broadcasted_iota(jnp.int32, scores[0].shape, 1)
    blk = lax.broadcasted_iota(jnp.int32, gates[0].shape, 0)
    stats = []
    for hh in range(MOBA_HEADS):
        s = jnp.where(kpos <= qpos, scores[hh], -jnp.inf)
        m0 = jnp.max(s, axis=0, keepdims=True)
        p = jnp.exp2(s - m0)
        stats.append((m0, jnp.sum(p, axis=0, keepdims=True)))
        acc_scr[hh] = jnp.dot(vtbf[hh, c], p.astype(BF16), preferred_element_type=F32)
    for hh in range(MOBA_HEADS):
        sel_scr[hh] = _top_blocks(gates[hh], blk, c, 0)

    def reduce(buf, n, carry):
        new = []
        for hh in range(MOBA_HEADS):
            m, l = carry[hh]
            s = jnp.where(sel_scr[hh, pl.ds(n, 1), :] > 0.0, buf[hh], -jnp.inf)
            m_new = jnp.maximum(m, jnp.max(s, axis=0, keepdims=True))
            alpha = jnp.exp2(m - m_new)
            p = jnp.exp2(s - m_new)
            acc_scr[hh] = alpha * acc_scr[hh] + jnp.dot(vtbf[hh, n], p.astype(BF16), preferred_element_type=F32)
            new.append((m_new, alpha * l + jnp.sum(p, axis=0, keepdims=True)))
        return tuple(new)

    def past(i, carry):
        n = 2 * i
        score(s_odd, n + 1)
        carry = reduce(s_even, n, carry)
        score(s_even, jnp.minimum(n + 2, n_blocks - 1))
        return reduce(s_odd, n + 1, carry)

    stats = lax.fori_loop(0, (c + 1) // 2, past, tuple(stats))
    for hh in range(MOBA_HEADS):
        lanes = slice(hh * HEAD_DIM, (hh + 1) * HEAD_DIM)
        o_ref[:, lanes] = ((acc_scr[hh] / stats[hh][1]).T * _silu(g_ref[:, lanes])).astype(o_ref.dtype)


def _moba(z, batch, t_len):
    assert MOBA_HEADS == COL_HEADS
    m = z.shape[0]
    n_blocks = t_len // MOBA_BLOCK

    def rows(j0):
        return pl.BlockSpec((MOBA_BLOCK, COL), lambda b, h, c: (b * n_blocks + c, j0 + h))

    def seq(j0):
        return pl.BlockSpec((t_len, COL), lambda b, h, c: (b, j0 + h))

    return pl.pallas_call(
        functools.partial(_moba_kernel, n_blocks=n_blocks),
        out_shape=jax.ShapeDtypeStruct((m, W_ATTN), BF16),
        grid=(batch, N_HEADS // MOBA_HEADS, n_blocks),
        in_specs=[rows(J_Q), seq(J_K), seq(J_V), rows(J_G)],
        out_specs=rows(0),
        scratch_shapes=[pltpu.VMEM((MOBA_HEADS, n_blocks, MOBA_BLOCK, HEAD_DIM), BF16),
                        pltpu.VMEM((MOBA_HEADS, n_blocks, HEAD_DIM, MOBA_BLOCK), BF16),
                        pltpu.VMEM((MOBA_HEADS, n_blocks, HEAD_DIM), F32),
                        pltpu.VMEM((MOBA_HEADS, n_blocks, MOBA_BLOCK), F32),
                        pltpu.VMEM((MOBA_HEADS, HEAD_DIM, MOBA_BLOCK), F32),
                        pltpu.VMEM((MOBA_HEADS, MOBA_BLOCK, HEAD_DIM), BF16),
                        pltpu.VMEM((MOBA_HEADS, MOBA_BLOCK, MOBA_BLOCK), F32),
                        pltpu.VMEM((MOBA_HEADS, MOBA_BLOCK, MOBA_BLOCK), F32)],
        compiler_params=pltpu.CompilerParams(
            dimension_semantics=("parallel", "parallel", "arbitrary"), vmem_limit_bytes=VMEM_LIMIT),
        name="moba",
    )(z, z, z, z)


def _head_rows(x4):
    rows = jnp.concatenate(
        [jnp.broadcast_to(x4[s:s + 1, :], (N_HEADS, W_ATTN)) for s in range(x4.shape[0])], axis=0)
    row_h = lax.broadcasted_iota(jnp.int32, rows.shape, 0) % N_HEADS
    lane_h = lax.broadcasted_iota(jnp.int32, rows.shape, 1) // HEAD_DIM
    return jnp.where(row_h == lane_h, rows, 0.0)


def _gather_heads(ref, n_rows):
    return jnp.concatenate([ref[pl.ds(h, n_rows, stride=N_HEADS), :] for h in range(N_HEADS)], axis=-1)


def _dec_qk_kernel(pt_ref, *refs, n_steps, dec_seq):
    pages = refs[:PAGES_PER_STEP]
    q_ref, kn_ref, vn_ref, p_ref, own_ref, inv_ref, kmean, wq = refs[PAGES_PER_STEP:]
    st = pl.program_id(1)
    n_rows = dec_seq * N_HEADS
    n_pages = n_steps * PAGES_PER_STEP
    pages_per_block = MOBA_BLOCK // PAGE_SIZE
    blocks_per_step = PAGES_PER_STEP // pages_per_block
    n_blocks = n_pages // pages_per_block

    @pl.when(st == 0)
    def _():
        wq[...] = _head_rows(q_ref[0])

    w = wq[...]
    for i in range(PAGES_PER_STEP):
        page = _gather_heads(pages[i], PAGE_SIZE)
        pagesum = jnp.sum(page.reshape(PAGE_SIZE // 8, 8, W_ATTN), axis=0)
        blocksum = pagesum if i % pages_per_block == 0 else blocksum + pagesum
        if i % pages_per_block == pages_per_block - 1:
            kmean[pl.ds(st * blocks_per_step + i // pages_per_block, 1), :] = (
                jnp.sum(blocksum, axis=0, keepdims=True) * (1.0 / MOBA_BLOCK))
        p_ref[0, st * PAGES_PER_STEP + i] = (
            lax.dot_general(w, page, _NT, preferred_element_type=F32) * ATTN_SCALE)

    @pl.when(st == n_steps - 1)
    def _():
        gate = lax.dot_general(w, kmean[...], _NT, precision=lax.Precision.HIGHEST,
                               preferred_element_type=F32)
        sel = _top_blocks_by_rank(gate)
        lo = lax.dot_general(w, kn_ref[0], _NT, preferred_element_type=F32) * ATTN_SCALE
        tok = lax.broadcasted_iota(jnp.int32, lo.shape, 0) // N_HEADS
        key = lax.broadcasted_iota(jnp.int32, lo.shape, 1)
        lo = jnp.where(key <= tok, lo, -jnp.inf)
        tops = [jnp.full((n_rows, PAGE_SIZE), -jnp.inf, F32) for _ in range(pages_per_block)]
        for pg in range(n_pages):
            kept = sel[:, pg // pages_per_block:pg // pages_per_block + 1] > 0.0
            tops[pg % pages_per_block] = jnp.maximum(tops[pg % pages_per_block],
                                                     jnp.where(kept, p_ref[0, pg], -jnp.inf))
        m = jnp.maximum(jnp.max(lo, axis=1, keepdims=True),
                        jnp.max(functools.reduce(jnp.maximum, tops), axis=1, keepdims=True))
        p_own = jnp.exp(lo - m)
        sums = [jnp.zeros((n_rows, PAGE_SIZE), F32) for _ in range(pages_per_block)]
        for pg in range(n_pages):
            kept = sel[:, pg // pages_per_block:pg // pages_per_block + 1] > 0.0
            e = jnp.where(kept, jnp.exp(p_ref[0, pg] - m), 0.0)
            p_ref[0, pg] = e
            sums[pg % pages_per_block] = sums[pg % pages_per_block] + e
        total = jnp.sum(sum(sums), axis=1, keepdims=True) + jnp.sum(p_own, axis=1, keepdims=True)
        inv_ref[0] = jnp.broadcast_to(1.0 / total, (n_rows, HEAD_DIM))
        vn = vn_ref[0]
        own = p_own[:, 0:1] * vn[0:1, :]
        for jj in range(1, dec_seq):
            own = own + p_own[:, jj:jj + 1] * vn[jj:jj + 1, :]
        own_ref[0] = own


def _page_spec(layer, i, n_pages):
    return pl.BlockSpec(
        (None, None, PAGE_SIZE * N_HEADS, HEAD_DIM),
        lambda b, st, pt: (layer, pt[b * n_pages + st * PAGES_PER_STEP + i], 0, 0))


def _dec_qk(pt_flat, cache_k, layer, q, kn, vn):
    db, dec_seq, _ = q.shape
    n_pages = pt_flat.shape[0] // db
    n_steps = n_pages // PAGES_PER_STEP
    n_rows = dec_seq * N_HEADS
    n_blocks = n_pages * PAGE_SIZE // MOBA_BLOCK
    tok = pl.BlockSpec((1, dec_seq, W_ATTN), lambda b, st, pt: (b, 0, 0))
    grid_spec = pltpu.PrefetchScalarGridSpec(
        num_scalar_prefetch=1,
        grid=(db, n_steps),
        in_specs=[_page_spec(layer, i, n_pages) for i in range(PAGES_PER_STEP)] + [tok, tok, tok],
        out_specs=(pl.BlockSpec((1, n_pages, n_rows, PAGE_SIZE), lambda b, st, pt: (b, 0, 0, 0)),
                   pl.BlockSpec((1, n_rows, W_ATTN), lambda b, st, pt: (b, 0, 0)),
                   pl.BlockSpec((1, n_rows, HEAD_DIM), lambda b, st, pt: (b, 0, 0))),
        scratch_shapes=[pltpu.VMEM((n_blocks, W_ATTN), F32),
                        pltpu.VMEM((n_rows, W_ATTN), F32)],
    )
    return pl.pallas_call(
        functools.partial(_dec_qk_kernel, n_steps=n_steps, dec_seq=dec_seq),
        out_shape=(jax.ShapeDtypeStruct((db, n_pages, n_rows, PAGE_SIZE), F32),
                   jax.ShapeDtypeStruct((db, n_rows, W_ATTN), F32),
                   jax.ShapeDtypeStruct((db, n_rows, HEAD_DIM), F32)),
        grid_spec=grid_spec,
        compiler_params=pltpu.CompilerParams(
            dimension_semantics=("parallel", "arbitrary"), vmem_limit_bytes=VMEM_LIMIT),
        name="dec_qk",
    )(pt_flat, *([cache_k] * PAGES_PER_STEP), q, kn, vn)


def _dec_pv_kernel(pt_ref, *refs, n_steps, dec_seq):
    pages = refs[:PAGES_PER_STEP]
    p_ref, own_ref, inv_ref, g_ref, o_ref, acc = refs[PAGES_PER_STEP:]
    st = pl.program_id(1)

    @pl.when(st == 0)
    def _():
        acc[...] = own_ref[0]

    a = acc[...]
    for i in range(PAGES_PER_STEP):
        a = a + jnp.dot(p_ref[0, st * PAGES_PER_STEP + i], _gather_heads(pages[i], PAGE_SIZE),
                        preferred_element_type=F32)
    acc[...] = a

    @pl.when(st == n_steps - 1)
    def _():
        row_h = lax.broadcasted_iota(jnp.int32, a.shape, 0) % N_HEADS
        lane_h = lax.broadcasted_iota(jnp.int32, a.shape, 1) // HEAD_DIM
        diag = jnp.where(row_h == lane_h, a * inv_ref[0][:, 0:1], 0.0)
        heads = jnp.concatenate(
            [jnp.sum(diag[s * N_HEADS:(s + 1) * N_HEADS], axis=0, keepdims=True) for s in range(dec_seq)], axis=0)
        o_ref[0] = heads * _silu(g_ref[0])


def _dec_pv(pt_flat, cache_v, layer, probs, own, inv, g):
    db, n_pages, n_rows, _ = probs.shape
    dec_seq = n_rows // N_HEADS
    n_steps = n_pages // PAGES_PER_STEP
    tok = pl.BlockSpec((1, dec_seq, W_ATTN), lambda b, st, pt: (b, 0, 0))
    grid_spec = pltpu.PrefetchScalarGridSpec(
        num_scalar_prefetch=1,
        grid=(db, n_steps),
        in_specs=[_page_spec(layer, i, n_pages) for i in range(PAGES_PER_STEP)] + [
            pl.BlockSpec((1, n_pages, n_rows, PAGE_SIZE), lambda b, st, pt: (b, 0, 0, 0)),
            pl.BlockSpec((1, n_rows, W_ATTN), lambda b, st, pt: (b, 0, 0)),
            pl.BlockSpec((1, n_rows, HEAD_DIM), lambda b, st, pt: (b, 0, 0)),
            tok],
        out_specs=tok,
        scratch_shapes=[pltpu.VMEM((n_rows, W_ATTN), F32)],
    )
    return pl.pallas_call(
        functools.partial(_dec_pv_kernel, n_steps=n_steps, dec_seq=dec_seq),
        out_shape=jax.ShapeDtypeStruct((db, dec_seq, W_ATTN), F32),
        grid_spec=grid_spec,
        compiler_params=pltpu.CompilerParams(
            dimension_semantics=("parallel", "arbitrary"), vmem_limit_bytes=VMEM_LIMIT),
        name="dec_pv",
    )(pt_flat, *([cache_v] * PAGES_PER_STEP), probs, own, inv, g)


def _dec_attention(pt_flat, cache_k, cache_v, layer, q, kn, vn, g):
    depth, n_phys = cache_k.shape[:2]
    view = (depth, n_phys, PAGE_SIZE * N_HEADS, HEAD_DIM)
    probs, own, inv = _dec_qk(pt_flat, cache_k.reshape(view), layer, q, kn, vn)
    return _dec_pv(pt_flat, cache_v.reshape(view), layer, probs, own, inv, g)


def _outproj_kernel(ab_ref, c_ref, x_ref, wa_ref, wc_ref, y_ref):
    y = jnp.dot(ab_ref[...].astype(BF16), wa_ref[...], preferred_element_type=F32)
    y = y + jnp.dot(c_ref[...].astype(BF16), wc_ref[...], preferred_element_type=F32)
    y_ref[...] = x_ref[...] + y


def _outproj(mix_ab, mix_c, x, w_out_bf, layer, tm):
    m = x.shape[0]
    half = W_CONV + W_POOL
    return pl.pallas_call(
        _outproj_kernel,
        out_shape=jax.ShapeDtypeStruct((m, D_MODEL), F32),
        grid=(m // tm, D_MODEL // COL),
        in_specs=[pl.BlockSpec((tm, half), lambda i, j: (i, 0)),
                  pl.BlockSpec((tm, W_ATTN), lambda i, j: (i, 0)),
                  pl.BlockSpec((tm, COL), lambda i, j: (i, j)),
                  pl.BlockSpec((None, half, COL), lambda i, j: (layer, 0, j)),
                  pl.BlockSpec((None, W_ATTN, COL), lambda i, j: (layer, 1, j))],
        out_specs=pl.BlockSpec((tm, COL), lambda i, j: (i, j)),
        compiler_params=pltpu.CompilerParams(
            dimension_semantics=("parallel", "arbitrary"), vmem_limit_bytes=VMEM_LIMIT),
        name="outproj",
    )(mix_ab, mix_c, x, w_out_bf, w_out_bf)


def _rope_tables(pos):
    half = HEAD_DIM // 2
    inv = ROPE_THETA ** (-jnp.arange(half, dtype=F32) / half)
    ang = pos.astype(F32)[:, None] * inv[None, :]
    cos, sin = jnp.cos(ang), jnp.sin(ang)
    return jnp.concatenate([cos, cos], axis=-1), jnp.concatenate([-sin, sin], axis=-1)


def kernel(x_prompt, x_sample, cache_k, cache_v, state_conv, state_pool, page_table, norm_g, w_in, w_out,
           conv_w, conv_b, conv_ln_g, conv_ln_b, pool_w, pool_scale, q_norm_g, k_norm_g):
    batch, seq, _ = x_prompt.shape
    db, dec_seq, _ = x_sample.shape
    depth = w_in.shape[0]
    past_len = page_table.shape[1] * PAGE_SIZE

    hp = x_prompt.reshape(batch * seq, D_MODEL)
    hs = x_sample.reshape(db * dec_seq, D_MODEL)
    cos_p, sin_p = _rope_tables(jnp.arange(seq, dtype=jnp.int32))
    cos_s, sin_s = _rope_tables(past_len + jnp.arange(db * dec_seq, dtype=jnp.int32) % dec_seq)
    pt_flat = page_table.reshape(-1)
    zero_cctx = jnp.zeros((batch, CONV_CTX, W_CONV), F32)
    zero_pctx = jnp.zeros((batch, POOL_CTX, W_POOL), F32)
    w_out_bf = w_out.astype(BF16)

    states = [[] for _ in range(4)]
    kv_p = kv_s = None
    for l in range(depth):
        pool_w_bf = pool_w[l].astype(BF16)
        small = (conv_w[l], conv_b[l][None], conv_ln_g[l][None], conv_ln_b[l][None], pool_w_bf, pool_scale[l][None])
        ng, qg, kg = norm_g[l][None], q_norm_g[l][None], k_norm_g[l][None]

        z, *kv_p = _proj(hp, ng, w_in, cos_p, sin_p, qg, kg, tm=1024, layer=l, depth=depth, kv_fin=kv_p)
        mix_ab, c_new, p_new = _convpool(z.reshape(batch, seq, D_IN), zero_cctx, zero_pctx, *small,
                                         tt=256, rows=64, pos0=0, mix_dtype=BF16)
        mix_c = _moba(z, batch, seq)
        hp = _outproj(mix_ab.reshape(batch * seq, -1), mix_c, hp, w_out_bf, l, tm=1024)
        states[0].append(c_new)
        states[1].append(p_new)

        z, *kv_s = _proj(hs, ng, w_in, cos_s, sin_s, qg, kg, tm=db * dec_seq, layer=l, depth=depth, kv_fin=kv_s)
        z = z.reshape(db, dec_seq, D_IN)
        mix_ab, c_new, p_new = _convpool(z, state_conv[l], state_pool[l], *small,
                                         tt=dec_seq, rows=dec_seq, pos0=past_len, nb=DEC_SEQS_PER_STEP)
        q3, k3, v3, g3 = (z[..., j0 * COL:j0 * COL + W_ATTN] for j0 in (J_Q, J_K, J_V, J_G))
        mix_c = _dec_attention(pt_flat, cache_k, cache_v, l, q3, k3, v3, g3)
        hs = _outproj(mix_ab.reshape(db * dec_seq, -1), mix_c.reshape(db * dec_seq, W_ATTN), hs, w_out_bf,
                      l, tm=db * dec_seq)
        states[2].append(c_new)
        states[3].append(p_new)

    kv_shape_p = (depth, batch, seq, N_HEADS, HEAD_DIM)
    kv_shape_s = (depth, db, dec_seq, N_HEADS, HEAD_DIM)
    return (hp.reshape(batch, seq, D_MODEL), hs.reshape(db, dec_seq, D_MODEL),
            kv_p[0].reshape(kv_shape_p), kv_p[1].reshape(kv_shape_p),
            jnp.stack(states[0]), jnp.stack(states[1]),
            kv_s[0].reshape(kv_shape_s), kv_s[1].reshape(kv_shape_s),
            jnp.stack(states[2]), jnp.stack(states[3]))
```

```python
import functools
import math

import jax
import jax.numpy as jnp
from jax import lax
from jax.experimental import pallas as pl
from jax.experimental.pallas import tpu as pltpu

F32 = jnp.float32
BF16 = jnp.bfloat16

D_MODEL = 2048
N_HEADS = 8
HEAD_DIM = 128
W_ATTN = N_HEADS * HEAD_DIM
W_CONV = 512
W_POOL = 512
CONV_WIDTH = 31
CONV_CTX = CONV_WIDTH - 1
POOL_WINDOWS = (2, 4, 8, 16)
POOL_GROUP = W_POOL // len(POOL_WINDOWS)
POOL_CTX = max(POOL_WINDOWS) - 1
MOBA_BLOCK = 256
MOBA_TOPK = 3
PAGE_SIZE = 128
ROPE_THETA = 10000.0
EPS = 1e-6
W_MISC = 3 * W_CONV + 2 * W_POOL
D_IN = W_MISC + 4 * W_ATTN
ATTN_SCALE = HEAD_DIM ** -0.5
LOG2E = math.log2(math.e)

COL = 512
COL_HEADS = COL // HEAD_DIM
N_COL = D_IN // COL
J_Q, J_K, J_V, J_G = 5, 7, 9, 11
CONV_PAD = 32
POOL_PAD = 16
PROJ_CHUNK = 256
MOBA_HEADS = 4
PAGES_PER_STEP = 32
DEC_SEQS_PER_STEP = 8
VMEM_LIMIT = 56 * 1024 * 1024

_NT = (((1,), (1,)), ((), ()))


def _silu(x):
    return x * jax.nn.sigmoid(x)


def _norm_rope(z, gain, cos, sin):
    outs = []
    for hh in range(COL_HEADS):
        xh = z[:, hh * HEAD_DIM:(hh + 1) * HEAD_DIM]
        ms = jnp.mean(xh * xh, axis=-1, keepdims=True)
        y = xh * lax.rsqrt(ms + EPS) * gain
        outs.append(y * cos + pltpu.roll(y, HEAD_DIM // 2, 1) * sin)
    return jnp.concatenate(outs, axis=-1)


def _store_heads(fin_ref, tile, first_row):
    rows = tile.shape[0]
    for hh in range(COL_HEADS):
        fin_ref[pl.ds(first_row + hh, rows, stride=N_HEADS), :] = tile[:, hh * HEAD_DIM:(hh + 1) * HEAD_DIM]


def _proj_kernel(*refs, aliased, emit_bf16, chunk):
    x_ref, ng_ref, w_ref, cos_ref, sin_ref, qg_ref, kg_ref = refs[:7]
    z_ref, kfin_ref, vfin_ref = refs[7 + 2 * aliased:10 + 2 * aliased]
    wbf_ref = refs[-2] if emit_bf16 else None
    h_scr = refs[-1]
    j = pl.program_id(1)
    tm = x_ref.shape[0]

    @pl.when(j == 0)
    def _():
        x = x_ref[...]
        ms = jnp.mean(x * x, axis=-1, keepdims=True)
        h_scr[...] = (x * lax.rsqrt(ms + EPS) * ng_ref[...]).astype(BF16)

    def columns(epilogue):
        w = w_ref[...].astype(BF16)
        if emit_bf16:
            wbf_ref[...] = w
        for r0 in range(0, tm, chunk):
            rows = slice(r0, r0 + chunk)
            z_ref[rows, :] = epilogue(jnp.dot(h_scr[rows, :], w, preferred_element_type=F32), r0)

    def rope_with(gain_ref):
        def epilogue(z, r0):
            return _norm_rope(z, gain_ref[...], cos_ref[r0:r0 + chunk, :], sin_ref[r0:r0 + chunk, :])
        return epilogue

    def copy_to(fin_ref, j0, inner=lambda z, r0: z):
        def epilogue(z, r0):
            z = inner(z, r0)
            _store_heads(fin_ref, z, r0 * N_HEADS + (j - j0) * COL_HEADS)
            return z
        return epilogue

    @pl.when((j < J_Q) | (j >= J_G))
    def _():
        columns(lambda z, r0: z)

    @pl.when((j >= J_Q) & (j < J_K))
    def _():
        columns(rope_with(qg_ref))

    @pl.when((j >= J_K) & (j < J_V))
    def _():
        columns(copy_to(kfin_ref, J_K, rope_with(kg_ref)))

    @pl.when((j >= J_V) & (j < J_G))
    def _():
        columns(copy_to(vfin_ref, J_V))


def _proj(x, norm_g, w_in, cos, sin, qg, kg, tm, layer=0, depth=1, kv_fin=None, emit_bf16=False):
    m = x.shape[0]
    n_pos_tiles = cos.shape[0] // tm
    aliased = kv_fin is not None
    fin_shape = jax.ShapeDtypeStruct((depth, m * N_HEADS, HEAD_DIM), F32)
    fin_spec = pl.BlockSpec((None, tm * N_HEADS, HEAD_DIM), lambda i, j: (layer, i, 0))
    in_specs = [
        pl.BlockSpec((tm, D_MODEL), lambda i, j: (i, 0)),
        pl.BlockSpec((1, D_MODEL), lambda i, j: (0, 0)),
        (pl.BlockSpec((None, D_MODEL, COL), lambda i, j: (layer, 0, j)) if w_in.ndim == 3
         else pl.BlockSpec((D_MODEL, COL), lambda i, j: (0, j))),
        pl.BlockSpec((tm, HEAD_DIM), lambda i, j: (i % n_pos_tiles, 0)),
        pl.BlockSpec((tm, HEAD_DIM), lambda i, j: (i % n_pos_tiles, 0)),
        pl.BlockSpec((1, HEAD_DIM), lambda i, j: (0, 0)),
        pl.BlockSpec((1, HEAD_DIM), lambda i, j: (0, 0)),
    ]
    args = [x, norm_g, w_in, cos, sin, qg, kg]
    aliases = {}
    if aliased:
        in_specs += [pl.BlockSpec(memory_space=pl.ANY), pl.BlockSpec(memory_space=pl.ANY)]
        args += list(kv_fin)
        aliases = {7: 1, 8: 2}
    out_shape = [jax.ShapeDtypeStruct((m, D_IN), F32), fin_shape, fin_shape]
    out_specs = [pl.BlockSpec((tm, COL), lambda i, j: (i, j)), fin_spec, fin_spec]
    if emit_bf16:
        assert m == tm
        out_shape.append(jax.ShapeDtypeStruct((D_MODEL, D_IN), BF16))
        out_specs.append(pl.BlockSpec((D_MODEL, COL), lambda i, j: (0, j)))
    return pl.pallas_call(
        functools.partial(_proj_kernel, aliased=aliased, emit_bf16=emit_bf16, chunk=min(tm, PROJ_CHUNK)),
        out_shape=tuple(out_shape),
        grid=(m // tm, N_COL),
        in_specs=in_specs,
        out_specs=tuple(out_specs),
        scratch_shapes=[pltpu.VMEM((tm, D_MODEL), BF16)],
        input_output_aliases=aliases,
        compiler_params=pltpu.CompilerParams(
            dimension_semantics=("parallel", "arbitrary"), vmem_limit_bytes=VMEM_LIMIT),
        name="proj",
    )(*args)


def _causal_conv(cs, cw_ref, cb_ref, base, rows, lane_group):
    lanes = slice(lane_group * HEAD_DIM, (lane_group + 1) * HEAD_DIM)
    acc = jnp.broadcast_to(cb_ref[:, lanes], (rows, HEAD_DIM))
    first = CONV_PAD - CONV_CTX
    for r in range(8):
        part = None
        for d in range(r, CONV_PAD + 1, 8):
            if d < first:
                continue
            term = cw_ref[d - first:d - first + 1, lanes] * cs[base + d - r:base + d - r + rows + 8, lanes]
            part = term if part is None else part + term
        acc = acc + part[r:r + rows]
    return acc


def _convpool_kernel(*refs, nb, **static):
    seq_in, shared, seq_out = refs[:7], refs[7:13], refs[13:]
    for bi in range(nb):
        _convpool_sequence(*(r.at[bi] for r in seq_in), *shared, *(r.at[bi] for r in seq_out), **static)


def _convpool_sequence(a_ref, b_ref, ga_ref, u_ref, gb_ref, cctx_ref, pctx_ref,
                       cw_ref, cb_ref, lg_ref, lb_ref, pw_ref, ps_ref,
                       mix_ref, cnew_ref, pnew_ref, cs, ps, *, tt, rows, n_t, pos0):
    t = pl.program_id(1)

    @pl.when(t == 0)
    def _():
        cs[0:CONV_PAD - CONV_CTX, :] = jnp.zeros((CONV_PAD - CONV_CTX, W_CONV), F32)
        cs[CONV_PAD - CONV_CTX:CONV_PAD, :] = cctx_ref[...]
        cs[CONV_PAD + tt:CONV_PAD + tt + 8, :] = jnp.zeros((8, W_CONV), F32)
        ps[0:POOL_PAD - POOL_CTX, :] = jnp.zeros((POOL_PAD - POOL_CTX, W_POOL), F32)
        ps[POOL_PAD - POOL_CTX:POOL_PAD, :] = pctx_ref[...]

    cs[CONV_PAD:CONV_PAD + tt, :] = a_ref[...] * jax.nn.sigmoid(b_ref[...])
    ps[POOL_PAD:POOL_PAD + tt, :] = u_ref[...]

    for base in range(0, tt, rows):
        acc = jnp.concatenate([_causal_conv(cs, cw_ref, cb_ref, base, rows, lg)
                               for lg in range(W_CONV // HEAD_DIM)], axis=-1)
        mu =jnp.mean(acc, axis=-1, keepdims=True)
        d = acc - mu
        var = jnp.mean(d * d, axis=-1, keepdims=True)
        ya = _silu(d * lax.rsqrt(var + EPS) * lg_ref[...] + lb_ref[...])
        mix_ref[pl.ds(base, rows), 0:W_CONV] = (
            ya * _silu(ga_ref[pl.ds(base, rows), :])).astype(mix_ref.dtype)
        pos = pos0 + t * tt + base + lax.broadcasted_iota(jnp.int32, (rows, 1), 0)
        for g, w in enumerate(POOL_WINDOWS):
            sl = slice(g * POOL_GROUP, (g + 1) * POOL_GROUP)
            tok = ps[pl.ds(base + POOL_PAD, rows), sl]
            s = tok
            for i in range(1, w):
                s = s + ps[pl.ds(base + POOL_PAD - i, rows), sl]
            cnt = jnp.minimum(w, pos + 1).astype(F32)
            p = s / cnt - tok
            yb = jnp.dot(p.astype(BF16), pw_ref[g], preferred_element_type=F32) * ps_ref[:, sl]
            mix_ref[pl.ds(base, rows), W_CONV + g * POOL_GROUP:W_CONV + (g + 1) * POOL_GROUP] = (
                yb * _silu(gb_ref[pl.ds(base, rows), sl])).astype(mix_ref.dtype)

    @pl.when(t == n_t - 1)
    def _():
        cnew_ref[...] = cs[tt + CONV_PAD - CONV_CTX:tt + CONV_PAD, :]
        pnew_ref[...] = ps[tt + POOL_PAD - POOL_CTX:tt + POOL_PAD, :]

    if n_t > 1:
        @pl.when(t < n_t - 1)
        def _():
            cs[0:CONV_PAD, :] = cs[tt:tt + CONV_PAD, :]
            ps[0:POOL_PAD, :] = ps[tt:tt + POOL_PAD, :]


def _convpool(misc, cctx, pctx, conv_w, conv_b, ln_g, ln_b, pool_w_bf, pool_scale, tt, rows, pos0, nb=1, mix_dtype=F32):
    n, t_len, _ = misc.shape
    n_t = t_len // tt

    def col(c):
        return pl.BlockSpec((nb, tt, COL), lambda b, t: (b, t, c))

    def full(shape):
        return pl.BlockSpec(shape, lambda b, t: (0,) * len(shape))

    return pl.pallas_call(
        functools.partial(_convpool_kernel, nb=nb, tt=tt, rows=rows, n_t=n_t, pos0=pos0),
        out_shape=(jax.ShapeDtypeStruct((n, t_len, W_CONV + W_POOL), mix_dtype),
                   jax.ShapeDtypeStruct((n, CONV_CTX, W_CONV), F32),
                   jax.ShapeDtypeStruct((n, POOL_CTX, W_POOL), F32)),
        grid=(n // nb, n_t),
        in_specs=[col(0), col(1), col(2), col(3), col(4),
                  pl.BlockSpec((nb, CONV_CTX, W_CONV), lambda b, t: (b, 0, 0)),
                  pl.BlockSpec((nb, POOL_CTX, W_POOL), lambda b, t: (b, 0, 0)),
                  full((CONV_WIDTH, W_CONV)), full((1, W_CONV)), full((1, W_CONV)), full((1, W_CONV)),
                  full((len(POOL_WINDOWS), POOL_GROUP, POOL_GROUP)), full((1, W_POOL))],
        out_specs=(pl.BlockSpec((nb, tt, W_CONV + W_POOL), lambda b, t: (b, t, 0)),
                   pl.BlockSpec((nb, CONV_CTX, W_CONV), lambda b, t: (b, 0, 0)),
                   pl.BlockSpec((nb, POOL_CTX, W_POOL), lambda b, t: (b, 0, 0))),
        scratch_shapes=[pltpu.VMEM((nb, CONV_PAD + tt + 8, W_CONV), F32),
                        pltpu.VMEM((nb, POOL_PAD + tt, W_POOL), F32)],
        compiler_params=pltpu.CompilerParams(
            dimension_semantics=("parallel", "arbitrary"), vmem_limit_bytes=VMEM_LIMIT),
        name="convpool",
    )(misc, misc, misc, misc, misc, cctx, pctx, conv_w, conv_b, ln_g, ln_b, pool_w_bf, pool_scale)


def _top_blocks(gate, idx, n_valid, axis):
    n_blocks = gate.shape[axis]
    gate = jnp.where(idx < n_valid, gate, -jnp.inf)
    sel = jnp.zeros(gate.shape, F32)
    for _ in range(MOBA_TOPK):
        best = jnp.max(gate, axis=axis, keepdims=True)
        first = jnp.min(jnp.where(gate == best, idx, n_blocks), axis=axis, keepdims=True)
        pick = idx == first
        sel = jnp.where(pick, 1.0, sel)
        gate = jnp.where(pick, -jnp.inf, gate)
    return jnp.where(idx < n_valid, sel, 0.0)


def _top_blocks_by_rank(gate):
    n_blocks = gate.shape[1]
    idx = lax.broadcasted_iota(jnp.int32, gate.shape, 1)
    rank = jnp.zeros(gate.shape, F32)
    for other in range(n_blocks):
        col = gate[:, other:other + 1]
        rank = rank + jnp.where(col > gate, 1.0, 0.0)
        rank = rank + jnp.where(col == gate, jnp.where(idx > other, 1.0, 0.0), 0.0)
    return jnp.where(rank < MOBA_TOPK, 1.0, 0.0)


def _moba_kernel(q_ref, k_ref, v_ref, g_ref, o_ref, kbf, vtbf, kmean, sel_scr, acc_scr, qs_scr, s_even, s_odd,
                 *, n_blocks):
    c = pl.program_id(2)

    @pl.when(c == 0)
    def _():
        def prep(n, carry):
            rows = pl.ds(pl.multiple_of(n * MOBA_BLOCK, MOBA_BLOCK), MOBA_BLOCK)
            for hh in range(MOBA_HEADS):
                lanes = slice(hh * HEAD_DIM, (hh + 1) * HEAD_DIM)
                kb = k_ref[rows, lanes]
                kbf[hh, n] = kb.astype(BF16)
                kmean[hh, pl.ds(n, 1), :] = jnp.sum(kb, axis=0, keepdims=True) * (1.0 / MOBA_BLOCK)
                vtbf[hh, n] = v_ref[rows, lanes].T.astype(BF16)
            return carry
        lax.fori_loop(0, n_blocks, prep, 0)

    qf = [q_ref[:, hh * HEAD_DIM:(hh + 1) * HEAD_DIM] for hh in range(MOBA_HEADS)]
    for hh in range(MOBA_HEADS):
        qs_scr[hh] = (qf[hh] * (ATTN_SCALE * LOG2E)).astype(BF16)
    scores = [lax.dot_general(kbf[hh, c], qs_scr[hh], _NT, preferred_element_type=F32)
              for hh in range(MOBA_HEADS)]

    def score(buf, n):
        for hh in range(MOBA_HEADS):
            buf[hh] = lax.dot_general(kbf[hh, n], qs_scr[hh], _NT, preferred_element_type=F32)

    score(s_even, 0)
    gates =[lax.dot_general(kmean[hh], qf[hh], _NT, precision=lax.Precision.HIGHEST,
                             preferred_element_type=F32) for hh in range(MOBA_HEADS)]
    kpos = lax.broadcasted_iota(jnp.int32, scores[0].shape, 0)
    qpos = lax.broadcasted_iota(jnp.int32, scores[0].shape, 1)
    blk = lax.broadcasted_iota(jnp.int32, gates[0].shape, 0)
    stats = []
    for hh in range(MOBA_HEADS):
        s = jnp.where(kpos <= qpos, scores[hh], -jnp.inf)
        m0 = jnp.max(s, axis=0, keepdims=True)
        p = jnp.exp2(s - m0)
        stats.append((m0, jnp.sum(p, axis=0, keepdims=True)))
        acc_scr[hh] = jnp.dot(vtbf[hh, c], p.astype(BF16), preferred_element_type=F32)
    for hh in range(MOBA_HEADS):
        sel_scr[hh] = _top_blocks(gates[hh], blk, c, 0)

    def reduce(buf, n, carry):
        new = []
        for hh in range(MOBA_HEADS):
            m, l = carry[hh]
            s = jnp.where(sel_scr[hh, pl.ds(n, 1), :] > 0.0, buf[hh], -jnp.inf)
            m_new = jnp.maximum(m, jnp.max(s, axis=0, keepdims=True))
            alpha = jnp.exp2(m - m_new)
            p = jnp.exp2(s - m_new)
            acc_scr[hh] = alpha * acc_scr[hh] + jnp.dot(vtbf[hh, n], p.astype(BF16), preferred_element_type=F32)
            new.append((m_new, alpha * l + jnp.sum(p, axis=0, keepdims=True)))
        return tuple(new)

    def past(i, carry):
        n = 2 * i
        score(s_odd, n + 1)
        carry = reduce(s_even, n, carry)
        score(s_even, jnp.minimum(n + 2, n_blocks - 1))
        return reduce(s_odd, n + 1, carry)

    stats = lax.fori_loop(0, (c + 1) // 2, past, tuple(stats))
    for hh in range(MOBA_HEADS):
        lanes = slice(hh * HEAD_DIM, (hh + 1) * HEAD_DIM)
        o_ref[:, lanes] = ((acc_scr[hh] / stats[hh][1]).T * _silu(g_ref[:, lanes])).astype(o_ref.dtype)


def _moba(z, batch, t_len):
    assert MOBA_HEADS == COL_HEADS
    m = z.shape[0]
    n_blocks = t_len // MOBA_BLOCK

    def rows(j0):
        return pl.BlockSpec((MOBA_BLOCK, COL), lambda b, h, c: (b * n_blocks + c, j0 + h))

    def seq(j0):
        return pl.BlockSpec((t_len, COL), lambda b, h, c: (b, j0 + h))

    return pl.pallas_call(
        functools.partial(_moba_kernel, n_blocks=n_blocks),
        out_shape=jax.ShapeDtypeStruct((m, W_ATTN), BF16),
        grid=(batch, N_HEADS // MOBA_HEADS, n_blocks),
        in_specs=[rows(J_Q), seq(J_K), seq(J_V), rows(J_G)],
        out_specs=rows(0),
        scratch_shapes=[pltpu.VMEM((MOBA_HEADS, n_blocks, MOBA_BLOCK, HEAD_DIM), BF16),
                        pltpu.VMEM((MOBA_HEADS, n_blocks, HEAD_DIM, MOBA_BLOCK), BF16),
                        pltpu.VMEM((MOBA_HEADS, n_blocks, HEAD_DIM), F32),
                        pltpu.VMEM((MOBA_HEADS, n_blocks, MOBA_BLOCK), F32),
                        pltpu.VMEM((MOBA_HEADS, HEAD_DIM, MOBA_BLOCK), F32),
                        pltpu.VMEM((MOBA_HEADS, MOBA_BLOCK, HEAD_DIM), BF16),
                        pltpu.VMEM((MOBA_HEADS, MOBA_BLOCK, MOBA_BLOCK), F32),
                        pltpu.VMEM((MOBA_HEADS, MOBA_BLOCK, MOBA_BLOCK), F32)],
        compiler_params=pltpu.CompilerParams(
            dimension_semantics=("parallel", "parallel", "arbitrary"), vmem_limit_bytes=VMEM_LIMIT),
        name="moba",
    )(z, z, z, z)


def _head_rows(x4):
    rows = jnp.concatenate(
        [jnp.broadcast_to(x4[s:s + 1, :], (N_HEADS, W_ATTN)) for s in range(x4.shape[0])], axis=0)
    row_h = lax.broadcasted_iota(jnp.int32, rows.shape, 0) % N_HEADS
    lane_h = lax.broadcasted_iota(jnp.int32, rows.shape, 1) // HEAD_DIM
    return jnp.where(row_h == lane_h, rows, 0.0)


def _gather_heads(ref, n_rows):
    return jnp.concatenate([ref[pl.ds(h, n_rows, stride=N_HEADS), :] for h in range(N_HEADS)], axis=-1)


def _dec_qk_kernel(pt_ref, *refs, n_steps, dec_seq):
    pages = refs[:PAGES_PER_STEP]
    q_ref, kn_ref, vn_ref, p_ref, own_ref, inv_ref, kmean, wq = refs[PAGES_PER_STEP:]
    st = pl.program_id(1)
    n_rows = dec_seq * N_HEADS
    n_pages = n_steps * PAGES_PER_STEP
    pages_per_block = MOBA_BLOCK // PAGE_SIZE
    blocks_per_step = PAGES_PER_STEP // pages_per_block
    n_blocks = n_pages // pages_per_block

    @pl.when(st == 0)
    def _():
        wq[...] = _head_rows(q_ref[0])

    w = wq[...]
    for i in range(PAGES_PER_STEP):
        page = _gather_heads(pages[i], PAGE_SIZE)
        pagesum = jnp.sum(page.reshape(PAGE_SIZE // 8, 8, W_ATTN), axis=0)
        blocksum = pagesum if i % pages_per_block == 0 else blocksum + pagesum
        if i % pages_per_block == pages_per_block - 1:
            kmean[pl.ds(st * blocks_per_step + i // pages_per_block, 1), :] = (
                jnp.sum(blocksum, axis=0, keepdims=True) * (1.0 / MOBA_BLOCK))
        p_ref[0, st * PAGES_PER_STEP + i] = (
            lax.dot_general(w, page, _NT, preferred_element_type=F32) * ATTN_SCALE)

    @pl.when(st == n_steps - 1)
    def _():
        gate = lax.dot_general(w, kmean[...], _NT, precision=lax.Precision.HIGHEST,
                               preferred_element_type=F32)
        sel = _top_blocks_by_rank(gate)
        lo = lax.dot_general(w, kn_ref[0], _NT, preferred_element_type=F32) * ATTN_SCALE
        tok = lax.broadcasted_iota(jnp.int32, lo.shape, 0) // N_HEADS
        key = lax.broadcasted_iota(jnp.int32, lo.shape, 1)
        lo = jnp.where(key <= tok, lo, -jnp.inf)
        tops = [jnp.full((n_rows, PAGE_SIZE), -jnp.inf, F32) for _ in range(pages_per_block)]
        for pg in range(n_pages):
            kept = sel[:, pg // pages_per_block:pg // pages_per_block + 1] > 0.0
            tops[pg % pages_per_block] = jnp.maximum(tops[pg % pages_per_block],
                                                     jnp.where(kept, p_ref[0, pg], -jnp.inf))
        m = jnp.maximum(jnp.max(lo, axis=1, keepdims=True),
                        jnp.max(functools.reduce(jnp.maximum, tops), axis=1, keepdims=True))
        p_own = jnp.exp(lo - m)
        sums = [jnp.zeros((n_rows, PAGE_SIZE), F32) for _ in range(pages_per_block)]
        for pg in range(n_pages):
            kept = sel[:, pg // pages_per_block:pg // pages_per_block + 1] > 0.0
            e = jnp.where(kept, jnp.exp(p_ref[0, pg] - m), 0.0)
            p_ref[0, pg] = e
            sums[pg % pages_per_block] = sums[pg % pages_per_block] + e
        total = jnp.sum(sum(sums), axis=1, keepdims=True) + jnp.sum(p_own, axis=1, keepdims=True)
        inv_ref[0] = jnp.broadcast_to(1.0 / total, (n_rows, HEAD_DIM))
        vn = vn_ref[0]
        own = p_own[:, 0:1] * vn[0:1, :]
        for jj in range(1, dec_seq):
            own = own + p_own[:, jj:jj + 1] * vn[jj:jj + 1, :]
        own_ref[0] = own


def _page_spec(layer, i, n_pages):
    return pl.BlockSpec(
        (None, None, PAGE_SIZE * N_HEADS, HEAD_DIM),
        lambda b, st, pt: (layer, pt[b * n_pages + st * PAGES_PER_STEP + i], 0, 0))


def _dec_qk(pt_flat, cache_k, layer, q, kn, vn):
    db, dec_seq, _ = q.shape
    n_pages = pt_flat.shape[0] // db
    n_steps = n_pages // PAGES_PER_STEP
    n_rows = dec_seq * N_HEADS
    n_blocks = n_pages * PAGE_SIZE // MOBA_BLOCK
    tok = pl.BlockSpec((1, dec_seq, W_ATTN), lambda b, st, pt: (b, 0, 0))
    grid_spec = pltpu.PrefetchScalarGridSpec(
        num_scalar_prefetch=1,
        grid=(db, n_steps),
        in_specs=[_page_spec(layer, i, n_pages) for i in range(PAGES_PER_STEP)] + [tok, tok, tok],
        out_specs=(pl.BlockSpec((1, n_pages, n_rows, PAGE_SIZE), lambda b, st, pt: (b, 0, 0, 0)),
                   pl.BlockSpec((1, n_rows, W_ATTN), lambda b, st, pt: (b, 0, 0)),
                   pl.BlockSpec((1, n_rows, HEAD_DIM), lambda b, st, pt: (b, 0, 0))),
        scratch_shapes=[pltpu.VMEM((n_blocks, W_ATTN), F32),
                        pltpu.VMEM((n_rows, W_ATTN), F32)],
    )
    return pl.pallas_call(
        functools.partial(_dec_qk_kernel, n_steps=n_steps, dec_seq=dec_seq),
        out_shape=(jax.ShapeDtypeStruct((db, n_pages, n_rows, PAGE_SIZE), F32),
                   jax.ShapeDtypeStruct((db, n_rows, W_ATTN), F32),
                   jax.ShapeDtypeStruct((db, n_rows, HEAD_DIM), F32)),
        grid_spec=grid_spec,
        compiler_params=pltpu.CompilerParams(
            dimension_semantics=("parallel", "arbitrary"), vmem_limit_bytes=VMEM_LIMIT),
        name="dec_qk",
    )(pt_flat, *([cache_k] * PAGES_PER_STEP), q, kn, vn)


def _dec_pv_kernel(pt_ref, *refs, n_steps, dec_seq):
    pages = refs[:PAGES_PER_STEP]
    p_ref, own_ref, inv_ref, g_ref, o_ref, acc = refs[PAGES_PER_STEP:]
    st = pl.program_id(1)

    @pl.when(st == 0)
    def _():
        acc[...] = own_ref[0]

    a = acc[...]
    for i in range(PAGES_PER_STEP):
        a = a + jnp.dot(p_ref[0, st * PAGES_PER_STEP + i], _gather_heads(pages[i], PAGE_SIZE),
                        preferred_element_type=F32)
    acc[...] = a

    @pl.when(st == n_steps - 1)
    def _():
        row_h = lax.broadcasted_iota(jnp.int32, a.shape, 0) % N_HEADS
        lane_h = lax.broadcasted_iota(jnp.int32, a.shape, 1) // HEAD_DIM
        diag = jnp.where(row_h == lane_h, a * inv_ref[0][:, 0:1], 0.0)
        heads = jnp.concatenate(
            [jnp.sum(diag[s * N_HEADS:(s + 1) * N_HEADS], axis=0, keepdims=True) for s in range(dec_seq)], axis=0)
        o_ref[0] = heads * _silu(g_ref[0])


def _dec_pv(pt_flat, cache_v, layer, probs, own, inv, g):
    db, n_pages, n_rows, _ = probs.shape
    dec_seq = n_rows // N_HEADS
    n_steps = n_pages // PAGES_PER_STEP
    tok = pl.BlockSpec((1, dec_seq, W_ATTN), lambda b, st, pt: (b, 0, 0))
    grid_spec = pltpu.PrefetchScalarGridSpec(
        num_scalar_prefetch=1,
        grid=(db, n_steps),
        in_specs=[_page_spec(layer, i, n_pages) for i in range(PAGES_PER_STEP)] + [
            pl.BlockSpec((1, n_pages, n_rows, PAGE_SIZE), lambda b, st, pt: (b, 0, 0, 0)),
            pl.BlockSpec((1, n_rows, W_ATTN), lambda b, st, pt: (b, 0, 0)),
            pl.BlockSpec((1, n_rows, HEAD_DIM), lambda b, st, pt: (b, 0, 0)),
            tok],
        out_specs=tok,
        scratch_shapes=[pltpu.VMEM((n_rows, W_ATTN), F32)],
    )
    return pl.pallas_call(
        functools.partial(_dec_pv_kernel, n_steps=n_steps, dec_seq=dec_seq),
        out_shape=jax.ShapeDtypeStruct((db, dec_seq, W_ATTN), F32),
        grid_spec=grid_spec,
        compiler_params=pltpu.CompilerParams(
            dimension_semantics=("parallel", "arbitrary"), vmem_limit_bytes=VMEM_LIMIT),
        name="dec_pv",
    )(pt_flat, *([cache_v] * PAGES_PER_STEP), probs, own, inv, g)


def _dec_attention(pt_flat, cache_k, cache_v, layer, q, kn, vn, g):
    depth, n_phys = cache_k.shape[:2]
    view = (depth, n_phys, PAGE_SIZE * N_HEADS, HEAD_DIM)
    probs, own, inv = _dec_qk(pt_flat, cache_k.reshape(view), layer, q, kn, vn)
    return _dec_pv(pt_flat, cache_v.reshape(view), layer, probs, own, inv, g)


def _outproj_kernel(ab_ref, c_ref, x_ref, wa_ref, wc_ref, y_ref):
    y = jnp.dot(ab_ref[...].astype(BF16), wa_ref[...].astype(BF16), preferred_element_type=F32)
    y = y + jnp.dot(c_ref[...].astype(BF16), wc_ref[...].astype(BF16), preferred_element_type=F32)
    y_ref[...] = x_ref[...] + y


def _outproj(mix_ab, mix_c, x, w_out, layer, tm):
    m = x.shape[0]
    half = W_CONV + W_POOL
    return pl.pallas_call(
        _outproj_kernel,
        out_shape=jax.ShapeDtypeStruct((m, D_MODEL), F32),
        grid=(m // tm, D_MODEL // COL),
        in_specs=[pl.BlockSpec((tm, half), lambda i, j: (i, 0)),
                  pl.BlockSpec((tm, W_ATTN), lambda i, j: (i, 0)),
                  pl.BlockSpec((tm, COL), lambda i, j: (i, j)),
                  pl.BlockSpec((None, half, COL), lambda i, j: (layer, 0, j)),
                  pl.BlockSpec((None, W_ATTN, COL), lambda i, j: (layer, 1, j))],
        out_specs=pl.BlockSpec((tm, COL), lambda i, j: (i, j)),
        compiler_params=pltpu.CompilerParams(
            dimension_semantics=("parallel", "arbitrary"), vmem_limit_bytes=VMEM_LIMIT),
        name="outproj",
    )(mix_ab, mix_c, x, w_out, w_out)


def _rope_tables(pos):
    half = HEAD_DIM // 2
    inv = ROPE_THETA ** (-jnp.arange(half, dtype=F32) / half)
    ang = pos.astype(F32)[:, None] * inv[None, :]
    cos, sin = jnp.cos(ang), jnp.sin(ang)
    return jnp.concatenate([cos, cos], axis=-1), jnp.concatenate([-sin, sin], axis=-1)


def kernel(x_prompt, x_sample, cache_k, cache_v, state_conv, state_pool, page_table, norm_g, w_in, w_out,
           conv_w, conv_b, conv_ln_g, conv_ln_b, pool_w, pool_scale, q_norm_g, k_norm_g):
    batch, seq, _ = x_prompt.shape
    db, dec_seq, _ = x_sample.shape
    depth = w_in.shape[0]
    past_len = page_table.shape[1] * PAGE_SIZE

    hp = x_prompt.reshape(batch * seq, D_MODEL)
    hs = x_sample.reshape(db * dec_seq, D_MODEL)
    cos_p, sin_p = _rope_tables(jnp.arange(seq, dtype=jnp.int32))
    cos_s, sin_s = _rope_tables(past_len + jnp.arange(db * dec_seq, dtype=jnp.int32) % dec_seq)
    pt_flat = page_table.reshape(-1)
    zero_cctx = jnp.zeros((batch, CONV_CTX, W_CONV), F32)
    zero_pctx = jnp.zeros((batch, POOL_CTX, W_POOL), F32)

    states = [[] for _ in range(4)]
    kv_p = kv_s = None
    for l in range(depth):
        pool_w_bf = pool_w[l].astype(BF16)
        small = (conv_w[l], conv_b[l][None], conv_ln_g[l][None], conv_ln_b[l][None], pool_w_bf, pool_scale[l][None])
        ng, qg, kg = norm_g[l][None], q_norm_g[l][None], k_norm_g[l][None]

        z_dec, *kv_s, w_in_bf = _proj(hs, ng, w_in, cos_s, sin_s, qg, kg, tm=db * dec_seq, layer=l, depth=depth,
                                      kv_fin=kv_s, emit_bf16=True)

        z, *kv_p = _proj(hp, ng, w_in_bf, cos_p, sin_p, qg, kg, tm=1024, layer=l, depth=depth, kv_fin=kv_p)
        mix_ab, c_new, p_new = _convpool(z.reshape(batch, seq, D_IN), zero_cctx, zero_pctx, *small,
                                         tt=256, rows=64, pos0=0, mix_dtype=BF16)
        mix_c = _moba(z, batch, seq)
        hp = _outproj(mix_ab.reshape(batch * seq, -1), mix_c, hp, w_out, l, tm=2048)
        states[0].append(c_new)
        states[1].append(p_new)

        z = z_dec.reshape(db, dec_seq, D_IN)
        mix_ab, c_new, p_new = _convpool(z, state_conv[l], state_pool[l], *small,
                                         tt=dec_seq, rows=dec_seq, pos0=past_len, nb=DEC_SEQS_PER_STEP)
        q3, k3, v3, g3 = (z[..., j0 * COL:j0 * COL + W_ATTN] for j0 in (J_Q, J_K, J_V, J_G))
        mix_c = _dec_attention(pt_flat, cache_k, cache_v, l, q3, k3, v3, g3)
        hs = _outproj(mix_ab.reshape(db * dec_seq, -1), mix_c.reshape(db * dec_seq, W_ATTN), hs, w_out,
                      l, tm=db * dec_seq)
        states[2].append(c_new)
        states[3].append(p_new)

    kv_shape_p = (depth, batch, seq, N_HEADS, HEAD_DIM)
    kv_shape_s = (depth, db, dec_seq, N_HEADS, HEAD_DIM)
    return (hp.reshape(batch, seq, D_MODEL), hs.reshape(db, dec_seq, D_MODEL),
            kv_p[0].reshape(kv_shape_p), kv_p[1].reshape(kv_shape_p),
            jnp.stack(states[0]), jnp.stack(states[1]),
            kv_s[0].reshape(kv_shape_s), kv_s[1].reshape(kv_shape_s),
            jnp.stack(states[2]), jnp.stack(states[3]))
```

```python
import functools
import math

import jax
import jax.numpy as jnp
from jax import lax
from jax.experimental import pallas as pl
from jax.experimental.pallas import tpu as pltpu

F32 = jnp.float32
BF16 = jnp.bfloat16

D_MODEL = 2048
N_HEADS = 8
HEAD_DIM = 128
W_ATTN = N_HEADS * HEAD_DIM
W_CONV = 512
W_POOL = 512
CONV_WIDTH = 31
CONV_CTX = CONV_WIDTH - 1
POOL_WINDOWS = (2, 4, 8, 16)
POOL_GROUP = W_POOL // len(POOL_WINDOWS)
POOL_CTX = max(POOL_WINDOWS) - 1
MOBA_BLOCK = 256
MOBA_TOPK = 3
PAGE_SIZE = 128
ROPE_THETA = 10000.0
EPS = 1e-6
W_MISC = 3 * W_CONV + 2 * W_POOL
D_IN = W_MISC + 4 * W_ATTN
ATTN_SCALE = HEAD_DIM ** -0.5
LOG2E = math.log2(math.e)

COL = 512
COL_HEADS = COL // HEAD_DIM
N_COL = D_IN // COL
J_Q, J_K, J_V, J_G = 5, 7, 9, 11
CONV_PAD = 32
POOL_PAD = 16
PROJ_CHUNK = 256
MOBA_HEADS = 4
PAGES_PER_STEP = 32
DEC_SEQS_PER_STEP = 8
VMEM_LIMIT = 56 * 1024 * 1024

_NT = (((1,), (1,)), ((), ()))


def _silu(x):
    return x * jax.nn.sigmoid(x)


def _norm_rope(z, gain, cos, sin):
    outs = []
    for hh in range(COL_HEADS):
        xh = z[:, hh * HEAD_DIM:(hh + 1) * HEAD_DIM]
        ms = jnp.mean(xh * xh, axis=-1, keepdims=True)
        y = xh * lax.rsqrt(ms + EPS) * gain
        outs.append(y * cos + pltpu.roll(y, HEAD_DIM // 2, 1) * sin)
    return jnp.concatenate(outs, axis=-1)


def _store_heads(fin_ref, tile, first_row):
    rows = tile.shape[0]
    for hh in range(COL_HEADS):
        fin_ref[pl.ds(first_row + hh, rows, stride=N_HEADS), :] = tile[:, hh * HEAD_DIM:(hh + 1) * HEAD_DIM]


def _proj_kernel(*refs, aliased, emit_bf16, chunk):
    x_ref, ng_ref, w_ref, cos_ref, sin_ref, qg_ref, kg_ref = refs[:7]
    z_ref, kfin_ref, vfin_ref = refs[7 + 2 * aliased:10 + 2 * aliased]
    wbf_ref = refs[-2] if emit_bf16 else None
    h_scr = refs[-1]
    j = pl.program_id(1)
    tm = x_ref.shape[0]

    @pl.when(j == 0)
    def _():
        x = x_ref[...]
        ms = jnp.mean(x * x, axis=-1, keepdims=True)
        h_scr[...] = (x * lax.rsqrt(ms + EPS) * ng_ref[...]).astype(BF16)

    def columns(epilogue):
        w = w_ref[...].astype(BF16)
        if emit_bf16:
            wbf_ref[...] = w
        for r0 in range(0, tm, chunk):
            rows = slice(r0, r0 + chunk)
            z_ref[rows, :] = epilogue(jnp.dot(h_scr[rows, :], w, preferred_element_type=F32), r0)

    def rope_with(gain_ref):
        def epilogue(z, r0):
            return _norm_rope(z, gain_ref[...], cos_ref[r0:r0 + chunk, :], sin_ref[r0:r0 + chunk, :])
        return epilogue

    def copy_to(fin_ref, j0, inner=lambda z, r0: z):
        def epilogue(z, r0):
            z = inner(z, r0)
            _store_heads(fin_ref, z, r0 * N_HEADS + (j - j0) * COL_HEADS)
            return z
        return epilogue

    @pl.when((j < J_Q) | (j >= J_G))
    def _():
        columns(lambda z, r0: z)

    @pl.when((j >= J_Q) & (j < J_K))
    def _():
        columns(rope_with(qg_ref))

    @pl.when((j >= J_K) & (j < J_V))
    def _():
        columns(copy_to(kfin_ref, J_K, rope_with(kg_ref)))

    @pl.when((j >= J_V) & (j < J_G))
    def _():
        columns(copy_to(vfin_ref, J_V))


def _proj(x, norm_g, w_in, cos, sin, qg, kg, tm, layer=0, depth=1, kv_fin=None, emit_bf16=False):
    m = x.shape[0]
    n_pos_tiles = cos.shape[0] // tm
    aliased = kv_fin is not None
    fin_shape = jax.ShapeDtypeStruct((depth, m * N_HEADS, HEAD_DIM), F32)
    fin_spec = pl.BlockSpec((None, tm * N_HEADS, HEAD_DIM), lambda i, j: (layer, i, 0))
    in_specs = [
        pl.BlockSpec((tm, D_MODEL), lambda i, j: (i, 0)),
        pl.BlockSpec((1, D_MODEL), lambda i, j: (0, 0)),
        (pl.BlockSpec((None, D_MODEL, COL), lambda i, j: (layer, 0, j)) if w_in.ndim == 3
         else pl.BlockSpec((D_MODEL, COL), lambda i, j: (0, j))),
        pl.BlockSpec((tm, HEAD_DIM), lambda i, j: (i % n_pos_tiles, 0)),
        pl.BlockSpec((tm, HEAD_DIM), lambda i, j: (i % n_pos_tiles, 0)),
        pl.BlockSpec((1, HEAD_DIM), lambda i, j: (0, 0)),
        pl.BlockSpec((1, HEAD_DIM), lambda i, j: (0, 0)),
    ]
    args = [x, norm_g, w_in, cos, sin, qg, kg]
    aliases = {}
    if aliased:
        in_specs += [pl.BlockSpec(memory_space=pl.ANY), pl.BlockSpec(memory_space=pl.ANY)]
        args += list(kv_fin)
        aliases = {7: 1, 8: 2}
    out_shape = [jax.ShapeDtypeStruct((m, D_IN), F32), fin_shape, fin_shape]
    out_specs = [pl.BlockSpec((tm, COL), lambda i, j: (i, j)), fin_spec, fin_spec]
    if emit_bf16:
        assert m == tm
        out_shape.append(jax.ShapeDtypeStruct((D_MODEL, D_IN), BF16))
        out_specs.append(pl.BlockSpec((D_MODEL, COL), lambda i, j: (0, j)))
    return pl.pallas_call(
        functools.partial(_proj_kernel, aliased=aliased, emit_bf16=emit_bf16, chunk=min(tm, PROJ_CHUNK)),
        out_shape=tuple(out_shape),
        grid=(m // tm, N_COL),
        in_specs=in_specs,
        out_specs=tuple(out_specs),
        scratch_shapes=[pltpu.VMEM((tm, D_MODEL), BF16)],
        input_output_aliases=aliases,
        compiler_params=pltpu.CompilerParams(
            dimension_semantics=("parallel", "arbitrary"), vmem_limit_bytes=VMEM_LIMIT),
        name="proj",
    )(*args)


def _causal_conv(cs, cw_ref, cb_ref, base, rows, lane_group):
    lanes = slice(lane_group * HEAD_DIM, (lane_group + 1) * HEAD_DIM)
    acc = jnp.broadcast_to(cb_ref[:, lanes], (rows, HEAD_DIM))
    first = CONV_PAD - CONV_CTX
    for r in range(8):
        part = None
        for d in range(r, CONV_PAD + 1, 8):
            if d < first:
                continue
            term = cw_ref[d - first:d - first + 1, lanes] * cs[base + d - r:base + d - r + rows + 8, lanes]
            part = term if part is None else part + term
        acc = acc + part[r:r + rows]
    return acc


def _convpool_kernel(*refs, nb, **static):
    seq_in, shared, seq_out = refs[:7], refs[7:13], refs[13:]
    for bi in range(nb):
        _convpool_sequence(*(r.at[bi] for r in seq_in), *shared, *(r.at[bi] for r in seq_out), **static)


def _convpool_sequence(a_ref, b_ref, ga_ref, u_ref, gb_ref, cctx_ref, pctx_ref,
                       cw_ref, cb_ref, lg_ref, lb_ref, pw_ref, ps_ref,
                       mix_ref, cnew_ref, pnew_ref, cs, ps, *, tt, rows, n_t, pos0):
    t = pl.program_id(1)

    @pl.when(t == 0)
    def _():
        cs[0:CONV_PAD - CONV_CTX, :] = jnp.zeros((CONV_PAD - CONV_CTX, W_CONV), F32)
        cs[CONV_PAD - CONV_CTX:CONV_PAD, :] = cctx_ref[...]
        cs[CONV_PAD + tt:CONV_PAD + tt + 8, :] = jnp.zeros((8, W_CONV), F32)
        ps[0:POOL_PAD - POOL_CTX, :] = jnp.zeros((POOL_PAD - POOL_CTX, W_POOL), F32)
        ps[POOL_PAD - POOL_CTX:POOL_PAD, :] = pctx_ref[...]

    cs[CONV_PAD:CONV_PAD + tt, :] = a_ref[...] * jax.nn.sigmoid(b_ref[...])
    ps[POOL_PAD:POOL_PAD + tt, :] = u_ref[...]

    for base in range(0, tt, rows):
        acc = jnp.concatenate([_causal_conv(cs, cw_ref, cb_ref, base, rows, lg)
                               for lg in range(W_CONV // HEAD_DIM)], axis=-1)
        mu =jnp.mean(acc, axis=-1, keepdims=True)
        d = acc - mu
        var = jnp.mean(d * d, axis=-1, keepdims=True)
        ya = _silu(d * lax.rsqrt(var + EPS) * lg_ref[...] + lb_ref[...])
        mix_ref[pl.ds(base, rows), 0:W_CONV] = (
            ya * _silu(ga_ref[pl.ds(base, rows), :])).astype(mix_ref.dtype)
        pos = pos0 + t * tt + base + lax.broadcasted_iota(jnp.int32, (rows, 1), 0)
        for g, w in enumerate(POOL_WINDOWS):
            sl = slice(g * POOL_GROUP, (g + 1) * POOL_GROUP)
            tok = ps[pl.ds(base + POOL_PAD, rows), sl]
            if rows % 8 == 0:
                s = ps[base:base + POOL_PAD + rows, sl]
                span = 1
                while span < w:
                    s = s + pltpu.roll(s, span, 0)
                    span *= 2
                s = s[POOL_PAD:POOL_PAD + rows]
            else:
                s = tok
                for i in range(1, w):
                    s = s + ps[pl.ds(base + POOL_PAD - i, rows), sl]
            cnt = jnp.minimum(w, pos + 1).astype(F32)
            p = s / cnt - tok
            yb = jnp.dot(p.astype(BF16), pw_ref[g], preferred_element_type=F32) * ps_ref[:, sl]
            mix_ref[pl.ds(base, rows), W_CONV + g * POOL_GROUP:W_CONV + (g + 1) * POOL_GROUP] = (
                yb * _silu(gb_ref[pl.ds(base, rows), sl])).astype(mix_ref.dtype)

    @pl.when(t == n_t - 1)
    def _():
        cnew_ref[...] = cs[tt + CONV_PAD - CONV_CTX:tt + CONV_PAD, :]
        pnew_ref[...] = ps[tt + POOL_PAD - POOL_CTX:tt + POOL_PAD, :]

    if n_t > 1:
        @pl.when(t < n_t - 1)
        def _():
            cs[0:CONV_PAD, :] = cs[tt:tt + CONV_PAD, :]
            ps[0:POOL_PAD, :] = ps[tt:tt + POOL_PAD, :]


def _convpool(misc, cctx, pctx, conv_w, conv_b, ln_g, ln_b, pool_w_bf, pool_scale, tt, rows, pos0, nb=1, mix_dtype=F32):
    n, t_len, _ = misc.shape
    n_t = t_len // tt

    def col(c):
        return pl.BlockSpec((nb, tt, COL), lambda b, t: (b, t, c))

    def full(shape):
        return pl.BlockSpec(shape, lambda b, t: (0,) * len(shape))

    return pl.pallas_call(
        functools.partial(_convpool_kernel, nb=nb, tt=tt, rows=rows, n_t=n_t, pos0=pos0),
        out_shape=(jax.ShapeDtypeStruct((n, t_len, W_CONV + W_POOL), mix_dtype),
                   jax.ShapeDtypeStruct((n, CONV_CTX, W_CONV), F32),
                   jax.ShapeDtypeStruct((n, POOL_CTX, W_POOL), F32)),
        grid=(n // nb, n_t),
        in_specs=[col(0), col(1), col(2), col(3), col(4),
                  pl.BlockSpec((nb, CONV_CTX, W_CONV), lambda b, t: (b, 0, 0)),
                  pl.BlockSpec((nb, POOL_CTX, W_POOL), lambda b, t: (b, 0, 0)),
                  full((CONV_WIDTH, W_CONV)), full((1, W_CONV)), full((1, W_CONV)), full((1, W_CONV)),
                  full((len(POOL_WINDOWS), POOL_GROUP, POOL_GROUP)), full((1, W_POOL))],
        out_specs=(pl.BlockSpec((nb, tt, W_CONV + W_POOL), lambda b, t: (b, t, 0)),
                   pl.BlockSpec((nb, CONV_CTX, W_CONV), lambda b, t: (b, 0, 0)),
                   pl.BlockSpec((nb, POOL_CTX, W_POOL), lambda b, t: (b, 0, 0))),
        scratch_shapes=[pltpu.VMEM((nb, CONV_PAD + tt + 8, W_CONV), F32),
                        pltpu.VMEM((nb, POOL_PAD + tt, W_POOL), F32)],
        compiler_params=pltpu.CompilerParams(
            dimension_semantics=("parallel", "arbitrary"), vmem_limit_bytes=VMEM_LIMIT),
        name="convpool",
    )(misc, misc, misc, misc, misc, cctx, pctx, conv_w, conv_b, ln_g, ln_b, pool_w_bf, pool_scale)


def _top_blocks(gate, idx, n_valid, axis):
    n_blocks = gate.shape[axis]
    gate = jnp.where(idx < n_valid, gate, -jnp.inf)
    sel = jnp.zeros(gate.shape, F32)
    for _ in range(MOBA_TOPK):
        best = jnp.max(gate, axis=axis, keepdims=True)
        first = jnp.min(jnp.where(gate == best, idx, n_blocks), axis=axis, keepdims=True)
        pick = idx == first
        sel = jnp.where(pick, 1.0, sel)
        gate = jnp.where(pick, -jnp.inf, gate)
    return jnp.where(idx < n_valid, sel, 0.0)


def _top_blocks_by_rank(gate):
    n_blocks = gate.shape[1]
    idx = lax.broadcasted_iota(jnp.int32, gate.shape, 1)
    rank = jnp.zeros(gate.shape, F32)
    for other in range(n_blocks):
        col = gate[:, other:other + 1]
        rank = rank + jnp.where(col > gate, 1.0, 0.0)
        rank = rank + jnp.where(col == gate, jnp.where(idx > other, 1.0, 0.0), 0.0)
    return jnp.where(rank < MOBA_TOPK, 1.0, 0.0)


def _moba_kernel(q_ref, k_ref, v_ref, g_ref, o_ref, kbf, vtbf, kmean, sel_scr, acc_scr, qs_scr, s_even, s_odd,
                 *, n_blocks):
    c = pl.program_id(2)

    @pl.when(c == 0)
    def _():
        def prep(n, carry):
            rows = pl.ds(pl.multiple_of(n * MOBA_BLOCK, MOBA_BLOCK), MOBA_BLOCK)
            for hh in range(MOBA_HEADS):
                lanes = slice(hh * HEAD_DIM, (hh + 1) * HEAD_DIM)
                kb = k_ref[rows, lanes]
                kbf[hh, n] = kb.astype(BF16)
                kmean[hh, pl.ds(n, 1), :] = jnp.sum(kb, axis=0, keepdims=True) * (1.0 / MOBA_BLOCK)
                vtbf[hh, n] = v_ref[rows, lanes].T.astype(BF16)
            return carry
        lax.fori_loop(0, n_blocks, prep, 0)

    qf = [q_ref[:, hh * HEAD_DIM:(hh + 1) * HEAD_DIM] for hh in range(MOBA_HEADS)]
    for hh in range(MOBA_HEADS):
        qs_scr[hh] = (qf[hh] * (ATTN_SCALE * LOG2E)).astype(BF16)
    scores = [lax.dot_general(kbf[hh, c], qs_scr[hh], _NT, preferred_element_type=F32)
              for hh in range(MOBA_HEADS)]

    def score(buf, n):
        for hh in range(MOBA_HEADS):
            buf[hh] = lax.dot_general(kbf[hh, n], qs_scr[hh], _NT, preferred_element_type=F32)

    score(s_even, 0)
    gates =[lax.dot_general(kmean[hh], qf[hh], _NT, precision=lax.Precision.HIGHEST,
                             preferred_element_type=F32) for hh in range(MOBA_HEADS)]
    kpos = lax.broadcasted_iota(jnp.int32, scores[0].shape, 0)
    qpos = lax.broadcasted_iota(jnp.int32, scores[0].shape, 1)
    blk = lax.broadcasted_iota(jnp.int32, gates[0].shape, 0)
    stats = []
    for hh in range(MOBA_HEADS):
        s = jnp.where(kpos <= qpos, scores[hh], -jnp.inf)
        m0 = jnp.max(s, axis=0, keepdims=True)
        p = jnp.exp2(s - m0)
        stats.append((m0, jnp.sum(p, axis=0, keepdims=True)))
        acc_scr[hh] = jnp.dot(vtbf[hh, c], p.astype(BF16), preferred_element_type=F32)
    for hh in range(MOBA_HEADS):
        sel_scr[hh] = _top_blocks(gates[hh], blk, c, 0)

    def reduce(buf, n, carry):
        new = []
        for hh in range(MOBA_HEADS):
            m, l = carry[hh]
            s = jnp.where(sel_scr[hh, pl.ds(n, 1), :] > 0.0, buf[hh], -jnp.inf)
            m_new = jnp.maximum(m, jnp.max(s, axis=0, keepdims=True))
            alpha = jnp.exp2(m - m_new)
            p = jnp.exp2(s - m_new)
            acc_scr[hh] = alpha * acc_scr[hh] + jnp.dot(vtbf[hh, n], p.astype(BF16), preferred_element_type=F32)
            new.append((m_new, alpha * l + jnp.sum(p, axis=0, keepdims=True)))
        return tuple(new)

    def past(i, carry):
        n = 2 * i
        score(s_odd, n + 1)
        carry = reduce(s_even, n, carry)
        score(s_even, jnp.minimum(n + 2, n_blocks - 1))
        return reduce(s_odd, n + 1, carry)

    stats = lax.fori_loop(0, (c + 1) // 2, past, tuple(stats))
    for hh in range(MOBA_HEADS):
        lanes = slice(hh * HEAD_DIM, (hh + 1) * HEAD_DIM)
        o_ref[:, lanes] = ((acc_scr[hh] / stats[hh][1]).T * _silu(g_ref[:, lanes])).astype(o_ref.dtype)


def _moba(z, batch, t_len):
    assert MOBA_HEADS == COL_HEADS
    m = z.shape[0]
    n_blocks = t_len // MOBA_BLOCK

    def rows(j0):
        return pl.BlockSpec((MOBA_BLOCK, COL), lambda b, h, c: (b * n_blocks + c, j0 + h))

    def seq(j0):
        return pl.BlockSpec((t_len, COL), lambda b, h, c: (b, j0 + h))

    return pl.pallas_call(
        functools.partial(_moba_kernel, n_blocks=n_blocks),
        out_shape=jax.ShapeDtypeStruct((m, W_ATTN), BF16),
        grid=(batch, N_HEADS // MOBA_HEADS, n_blocks),
        in_specs=[rows(J_Q), seq(J_K), seq(J_V), rows(J_G)],
        out_specs=rows(0),
        scratch_shapes=[pltpu.VMEM((MOBA_HEADS, n_blocks, MOBA_BLOCK, HEAD_DIM), BF16),
                        pltpu.VMEM((MOBA_HEADS, n_blocks, HEAD_DIM, MOBA_BLOCK), BF16),
                        pltpu.VMEM((MOBA_HEADS, n_blocks, HEAD_DIM), F32),
                        pltpu.VMEM((MOBA_HEADS, n_blocks, MOBA_BLOCK), F32),
                        pltpu.VMEM((MOBA_HEADS, HEAD_DIM, MOBA_BLOCK), F32),
                        pltpu.VMEM((MOBA_HEADS, MOBA_BLOCK, HEAD_DIM), BF16),
                        pltpu.VMEM((MOBA_HEADS, MOBA_BLOCK, MOBA_BLOCK), F32),
                        pltpu.VMEM((MOBA_HEADS, MOBA_BLOCK, MOBA_BLOCK), F32)],
        compiler_params=pltpu.CompilerParams(
            dimension_semantics=("parallel", "parallel", "arbitrary"), vmem_limit_bytes=VMEM_LIMIT),
        name="moba",
    )(z, z, z, z)


def _head_rows(x4):
    rows = jnp.concatenate(
        [jnp.broadcast_to(x4[s:s + 1, :], (N_HEADS, W_ATTN)) for s in range(x4.shape[0])], axis=0)
    row_h = lax.broadcasted_iota(jnp.int32, rows.shape, 0) % N_HEADS
    lane_h = lax.broadcasted_iota(jnp.int32, rows.shape, 1) // HEAD_DIM
    return jnp.where(row_h == lane_h, rows, 0.0)


def _gather_heads(ref, n_rows):
    return jnp.concatenate([ref[pl.ds(h, n_rows, stride=N_HEADS), :] for h in range(N_HEADS)], axis=-1)


def _dec_qk_kernel(pt_ref, *refs, n_steps, dec_seq):
    pages = refs[:PAGES_PER_STEP]
    q_ref, kn_ref, vn_ref, p_ref, own_ref, inv_ref, kmean, wq = refs[PAGES_PER_STEP:]
    st = pl.program_id(1)
    n_rows = dec_seq * N_HEADS
    n_pages = n_steps * PAGES_PER_STEP
    pages_per_block = MOBA_BLOCK // PAGE_SIZE
    blocks_per_step = PAGES_PER_STEP // pages_per_block
    n_blocks = n_pages // pages_per_block

    @pl.when(st == 0)
    def _():
        wq[...] = _head_rows(q_ref[0])

    w = wq[...]
    for i in range(PAGES_PER_STEP):
        page = _gather_heads(pages[i], PAGE_SIZE)
        pagesum = jnp.sum(page.reshape(PAGE_SIZE // 8, 8, W_ATTN), axis=0)
        blocksum = pagesum if i % pages_per_block == 0 else blocksum + pagesum
        if i % pages_per_block == pages_per_block - 1:
            kmean[pl.ds(st * blocks_per_step + i // pages_per_block, 1), :] = (
                jnp.sum(blocksum, axis=0, keepdims=True) * (1.0 / MOBA_BLOCK))
        p_ref[0, st * PAGES_PER_STEP + i] = (
            lax.dot_general(w, page, _NT, preferred_element_type=F32) * ATTN_SCALE)

    @pl.when(st == n_steps - 1)
    def _():
        gate = lax.dot_general(w, kmean[...], _NT, precision=lax.Precision.HIGHEST,
                               preferred_element_type=F32)
        sel = _top_blocks_by_rank(gate)
        lo = lax.dot_general(w, kn_ref[0], _NT, preferred_element_type=F32) * ATTN_SCALE
        tok = lax.broadcasted_iota(jnp.int32, lo.shape, 0) // N_HEADS
        key = lax.broadcasted_iota(jnp.int32, lo.shape, 1)
        lo = jnp.where(key <= tok, lo, -jnp.inf)
        tops = [jnp.full((n_rows, PAGE_SIZE), -jnp.inf, F32) for _ in range(pages_per_block)]
        for pg in range(n_pages):
            kept = sel[:, pg // pages_per_block:pg // pages_per_block + 1] > 0.0
            tops[pg % pages_per_block] = jnp.maximum(tops[pg % pages_per_block],
                                                     jnp.where(kept, p_ref[0, pg], -jnp.inf))
        m = jnp.maximum(jnp.max(lo, axis=1, keepdims=True),
                        jnp.max(functools.reduce(jnp.maximum, tops), axis=1, keepdims=True))
        p_own = jnp.exp(lo - m)
        sums = [jnp.zeros((n_rows, PAGE_SIZE), F32) for _ in range(pages_per_block)]
        for pg in range(n_pages):
            kept = sel[:, pg // pages_per_block:pg // pages_per_block + 1] > 0.0
            e = jnp.where(kept, jnp.exp(p_ref[0, pg] - m), 0.0)
            p_ref[0, pg] = e
            sums[pg % pages_per_block] = sums[pg % pages_per_block] + e
        total = jnp.sum(sum(sums), axis=1, keepdims=True) + jnp.sum(p_own, axis=1, keepdims=True)
        inv_ref[0] = jnp.broadcast_to(1.0 / total, (n_rows, HEAD_DIM))
        vn = vn_ref[0]
        own = p_own[:, 0:1] * vn[0:1, :]
        for jj in range(1, dec_seq):
            own = own + p_own[:, jj:jj + 1] * vn[jj:jj + 1, :]
        own_ref[0] = own


def _page_spec(layer, i, n_pages):
    return pl.BlockSpec(
        (None, None, PAGE_SIZE * N_HEADS, HEAD_DIM),
        lambda b, st, pt: (layer, pt[b * n_pages + st * PAGES_PER_STEP + i], 0, 0))


def _dec_qk(pt_flat, cache_k, layer, q, kn, vn):
    db, dec_seq, _ = q.shape
    n_pages = pt_flat.shape[0] // db
    n_steps = n_pages // PAGES_PER_STEP
    n_rows = dec_seq * N_HEADS
    n_blocks = n_pages * PAGE_SIZE // MOBA_BLOCK
    tok = pl.BlockSpec((1, dec_seq, W_ATTN), lambda b, st, pt: (b, 0, 0))
    grid_spec = pltpu.PrefetchScalarGridSpec(
        num_scalar_prefetch=1,
        grid=(db, n_steps),
        in_specs=[_page_spec(layer, i, n_pages) for i in range(PAGES_PER_STEP)] + [tok, tok, tok],
        out_specs=(pl.BlockSpec((1, n_pages, n_rows, PAGE_SIZE), lambda b, st, pt: (b, 0, 0, 0)),
                   pl.BlockSpec((1, n_rows, W_ATTN), lambda b, st, pt: (b, 0, 0)),
                   pl.BlockSpec((1, n_rows, HEAD_DIM), lambda b, st, pt: (b, 0, 0))),
        scratch_shapes=[pltpu.VMEM((n_blocks, W_ATTN), F32),
                        pltpu.VMEM((n_rows, W_ATTN), F32)],
    )
    return pl.pallas_call(
        functools.partial(_dec_qk_kernel, n_steps=n_steps, dec_seq=dec_seq),
        out_shape=(jax.ShapeDtypeStruct((db, n_pages, n_rows, PAGE_SIZE), F32),
                   jax.ShapeDtypeStruct((db, n_rows, W_ATTN), F32),
                   jax.ShapeDtypeStruct((db, n_rows, HEAD_DIM), F32)),
        grid_spec=grid_spec,
        compiler_params=pltpu.CompilerParams(
            dimension_semantics=("parallel", "arbitrary"), vmem_limit_bytes=VMEM_LIMIT),
        name="dec_qk",
    )(pt_flat, *([cache_k] * PAGES_PER_STEP), q, kn, vn)


def _dec_pv_kernel(pt_ref, *refs, n_steps, dec_seq):
    pages = refs[:PAGES_PER_STEP]
    p_ref, own_ref, inv_ref, g_ref, o_ref, acc = refs[PAGES_PER_STEP:]
    st = pl.program_id(1)

    @pl.when(st == 0)
    def _():
        acc[...] = own_ref[0]

    a = acc[...]
    for i in range(PAGES_PER_STEP):
        a = a + jnp.dot(p_ref[0, st * PAGES_PER_STEP + i], _gather_heads(pages[i], PAGE_SIZE),
                        preferred_element_type=F32)
    acc[...] = a

    @pl.when(st == n_steps - 1)
    def _():
        row_h = lax.broadcasted_iota(jnp.int32, a.shape, 0) % N_HEADS
        lane_h = lax.broadcasted_iota(jnp.int32, a.shape, 1) // HEAD_DIM
        diag = jnp.where(row_h == lane_h, a * inv_ref[0][:, 0:1], 0.0)
        heads = jnp.concatenate(
            [jnp.sum(diag[s * N_HEADS:(s + 1) * N_HEADS], axis=0, keepdims=True) for s in range(dec_seq)], axis=0)
        o_ref[0] = heads * _silu(g_ref[0])


def _dec_pv(pt_flat, cache_v, layer, probs, own, inv, g):
    db, n_pages, n_rows, _ = probs.shape
    dec_seq = n_rows // N_HEADS
    n_steps = n_pages // PAGES_PER_STEP
    tok = pl.BlockSpec((1, dec_seq, W_ATTN), lambda b, st, pt: (b, 0, 0))
    grid_spec = pltpu.PrefetchScalarGridSpec(
        num_scalar_prefetch=1,
        grid=(db, n_steps),
        in_specs=[_page_spec(layer, i, n_pages) for i in range(PAGES_PER_STEP)] + [
            pl.BlockSpec((1, n_pages, n_rows, PAGE_SIZE), lambda b, st, pt: (b, 0, 0, 0)),
            pl.BlockSpec((1, n_rows, W_ATTN), lambda b, st, pt: (b, 0, 0)),
            pl.BlockSpec((1, n_rows, HEAD_DIM), lambda b, st, pt: (b, 0, 0)),
            tok],
        out_specs=tok,
        scratch_shapes=[pltpu.VMEM((n_rows, W_ATTN), F32)],
    )
    return pl.pallas_call(
        functools.partial(_dec_pv_kernel, n_steps=n_steps, dec_seq=dec_seq),
        out_shape=jax.ShapeDtypeStruct((db, dec_seq, W_ATTN), F32),
        grid_spec=grid_spec,
        compiler_params=pltpu.CompilerParams(
            dimension_semantics=("parallel", "arbitrary"), vmem_limit_bytes=VMEM_LIMIT),
        name="dec_pv",
    )(pt_flat, *([cache_v] * PAGES_PER_STEP), probs, own, inv, g)


def _dec_attention(pt_flat, cache_k, cache_v, layer, q, kn, vn, g):
    depth, n_phys = cache_k.shape[:2]
    view = (depth, n_phys, PAGE_SIZE * N_HEADS, HEAD_DIM)
    probs, own, inv = _dec_qk(pt_flat, cache_k.reshape(view), layer, q, kn, vn)
    return _dec_pv(pt_flat, cache_v.reshape(view), layer, probs, own, inv, g)


def _outproj_kernel(ab_ref, c_ref, x_ref, wa_ref, wc_ref, y_ref):
    y = jnp.dot(ab_ref[...].astype(BF16), wa_ref[...].astype(BF16), preferred_element_type=F32)
    y = y + jnp.dot(c_ref[...].astype(BF16), wc_ref[...].astype(BF16), preferred_element_type=F32)
    y_ref[...] = x_ref[...] + y


def _outproj(mix_ab, mix_c, x, w_out, layer, tm):
    m = x.shape[0]
    half = W_CONV + W_POOL
    return pl.pallas_call(
        _outproj_kernel,
        out_shape=jax.ShapeDtypeStruct((m, D_MODEL), F32),
        grid=(m // tm, D_MODEL // COL),
        in_specs=[pl.BlockSpec((tm, half), lambda i, j: (i, 0)),
                  pl.BlockSpec((tm, W_ATTN), lambda i, j: (i, 0)),
                  pl.BlockSpec((tm, COL), lambda i, j: (i, j)),
                  pl.BlockSpec((None, half, COL), lambda i, j: (layer, 0, j)),
                  pl.BlockSpec((None, W_ATTN, COL), lambda i, j: (layer, 1, j))],
        out_specs=pl.BlockSpec((tm, COL), lambda i, j: (i, j)),
        compiler_params=pltpu.CompilerParams(
            dimension_semantics=("parallel", "arbitrary"), vmem_limit_bytes=VMEM_LIMIT),
        name="outproj",
    )(mix_ab, mix_c, x, w_out, w_out)


def _rope_tables(pos):
    half = HEAD_DIM // 2
    inv = ROPE_THETA ** (-jnp.arange(half, dtype=F32) / half)
    ang = pos.astype(F32)[:, None] * inv[None, :]
    cos, sin = jnp.cos(ang), jnp.sin(ang)
    return jnp.concatenate([cos, cos], axis=-1), jnp.concatenate([-sin, sin], axis=-1)


def kernel(x_prompt, x_sample, cache_k, cache_v, state_conv, state_pool, page_table, norm_g, w_in, w_out,
           conv_w, conv_b, conv_ln_g, conv_ln_b, pool_w, pool_scale, q_norm_g, k_norm_g):
    batch, seq, _ = x_prompt.shape
    db, dec_seq, _ = x_sample.shape
    depth = w_in.shape[0]
    past_len = page_table.shape[1] * PAGE_SIZE

    hp = x_prompt.reshape(batch * seq, D_MODEL)
    hs = x_sample.reshape(db * dec_seq, D_MODEL)
    cos_p, sin_p = _rope_tables(jnp.arange(seq, dtype=jnp.int32))
    cos_s, sin_s = _rope_tables(past_len + jnp.arange(db * dec_seq, dtype=jnp.int32) % dec_seq)
    pt_flat = page_table.reshape(-1)
    zero_cctx = jnp.zeros((batch, CONV_CTX, W_CONV), F32)
    zero_pctx = jnp.zeros((batch, POOL_CTX, W_POOL), F32)

    states = [[] for _ in range(4)]
    kv_p = kv_s = None
    for l in range(depth):
        pool_w_bf = pool_w[l].astype(BF16)
        small = (conv_w[l], conv_b[l][None], conv_ln_g[l][None], conv_ln_b[l][None], pool_w_bf, pool_scale[l][None])
        ng, qg, kg = norm_g[l][None], q_norm_g[l][None], k_norm_g[l][None]

        z_dec, *kv_s, w_in_bf = _proj(hs, ng, w_in, cos_s, sin_s, qg, kg, tm=db * dec_seq, layer=l, depth=depth,
                                      kv_fin=kv_s, emit_bf16=True)

        z, *kv_p = _proj(hp, ng, w_in_bf, cos_p, sin_p, qg, kg, tm=1024, layer=l, depth=depth, kv_fin=kv_p)
        mix_ab, c_new, p_new = _convpool(z.reshape(batch, seq, D_IN), zero_cctx, zero_pctx, *small,
                                         tt=256, rows=64, pos0=0, mix_dtype=BF16)
        mix_c = _moba(z, batch, seq)
        hp = _outproj(mix_ab.reshape(batch * seq, -1), mix_c, hp, w_out, l, tm=2048)
        states[0].append(c_new)
        states[1].append(p_new)

        z = z_dec.reshape(db, dec_seq, D_IN)
        mix_ab, c_new, p_new = _convpool(z, state_conv[l], state_pool[l], *small,
                                         tt=dec_seq, rows=dec_seq, pos0=past_len, nb=DEC_SEQS_PER_STEP)
        q3, k3, v3, g3 = (z[..., j0 * COL:j0 * COL + W_ATTN] for j0 in (J_Q, J_K, J_V, J_G))
        mix_c = _dec_attention(pt_flat, cache_k, cache_v, l, q3, k3, v3, g3)
        hs = _outproj(mix_ab.reshape(db * dec_seq, -1), mix_c.reshape(db * dec_seq, W_ATTN), hs, w_out,
                      l, tm=db * dec_seq)
        states[2].append(c_new)
        states[3].append(p_new)

    kv_shape_p = (depth, batch, seq, N_HEADS, HEAD_DIM)
    kv_shape_s = (depth, db, dec_seq, N_HEADS, HEAD_DIM)
    return (hp.reshape(batch, seq, D_MODEL), hs.reshape(db, dec_seq, D_MODEL),
            kv_p[0].reshape(kv_shape_p), kv_p[1].reshape(kv_shape_p),
            jnp.stack(states[0]), jnp.stack(states[1]),
            kv_s[0].reshape(kv_shape_s), kv_s[1].reshape(kv_shape_s),
            jnp.stack(states[2]), jnp.stack(states[3]))
```
